```python
import math
import jax
import jax.numpy as jnp
from jax import lax
import numpy as np

D_MODEL = 1024
BATCH = 4
SEQ = 8192
DEPTH = 2
DEC_BATCH = 16
DEC_SEQ = 64
PAST_LEN = 2048

CHUNK = 64
Q_BLOCK = 128
HEAD_DIM = 64
FOX_HEADS = 4
DIFF_HEADS = 4
DIFF_DK = 32
DIFF_DV = 64
SB_HEADS = 4
GDN_HEADS = 4
GDN_DK = 64
GDN_DV = 64
CONV_WIDTH = 4
GDN_CHUNK = CHUNK
N_BRANCH = 4
BRANCH_W = 256
D_FF = -(-(8 * D_MODEL) // (3 * 256)) * 256
EPS = 1e-6

GDN_CONV_CH = GDN_HEADS * (2 * GDN_DK + GDN_DV)
FOX_COLS = 3 * FOX_HEADS * HEAD_DIM + FOX_HEADS
DIFF_COLS = 2 * DIFF_HEADS * 2 * DIFF_DK + DIFF_HEADS * DIFF_DV
SB_COLS = 3 * SB_HEADS * HEAD_DIM
GDN_COLS = GDN_CONV_CH + 2 * GDN_HEADS + GDN_HEADS * GDN_DV
GATE_COLS = N_BRANCH * D_MODEL
N_IN = FOX_COLS + DIFF_COLS + SB_COLS + GDN_COLS + GATE_COLS
SPLITS = (FOX_COLS, FOX_COLS + DIFF_COLS, FOX_COLS + DIFF_COLS + SB_COLS, FOX_COLS + DIFF_COLS + SB_COLS + GDN_COLS)
F32 = jnp.float32

kernel_name = 'hybrid_streaming_encoder_step'


def rms_norm(x, g):
    xf = x.astype(F32)
    y = xf * lax.rsqrt(jnp.mean(xf * xf, axis=-1, keepdims=True) + EPS)
    return y.astype(x.dtype) * g


def l2norm(x):
    return x * lax.rsqrt(jnp.sum(x * x, axis=-1, keepdims=True) + EPS)


def _cat(past, new):
    return new if past is None else jnp.concatenate([past.astype(new.dtype), new], axis=1)


def _to_blocks(a, nb):
    return jnp.moveaxis(a.reshape((a.shape[0], nb, Q_BLOCK) + a.shape[2:]), 1, 0)


def _from_blocks(a):
    nb, b, qb = a.shape[:3]
    return jnp.moveaxis(a, 0, 1).reshape((b, nb * qb) + a.shape[3:])


def query_sweep(core, q_args, q_pos, kv_args):
    T = q_pos.shape[0]
    if T <= Q_BLOCK:
        return core(q_args, q_pos, *kv_args)
    nb = T // Q_BLOCK
    blocks = tuple(_to_blocks(a, nb) for a in q_args)
    out = lax.map(lambda xs: core(xs[0], xs[1], *kv_args), (blocks, q_pos.reshape(nb, Q_BLOCK)))
    return _from_blocks(out)


def fox_core(q_args, q_pos, k, v, ck, k_pos):
    q, cq = q_args
    s = jnp.einsum('bqhd,bkhd->bhqk', q, k).astype(F32) * (HEAD_DIM ** -0.5)
    s = s + jnp.swapaxes(cq, 1, 2)[..., :, None] - jnp.swapaxes(ck, 1, 2)[..., None, :]
    mask = k_pos[None, :] <= q_pos[:, None]
    p = jax.nn.softmax(jnp.where(mask, s, -jnp.inf), axis=-1)
    return jnp.einsum('bhqk,bkhd->bqhd', p.astype(v.dtype), v)


def diff_core(q_args, q_pos, k, v, k_pos, lam):
    (q,) = q_args
    s = jnp.einsum('bqhcd,bkhcd->bhcqk', q, k).astype(F32) * (DIFF_DK ** -0.5)
    mask = (k_pos // CHUNK)[None, :] <= (q_pos // CHUNK)[:, None]
    p = jax.nn.softmax(jnp.where(mask, s, -jnp.inf), axis=-1)
    a = p[:, :, 0] - lam * p[:, :, 1]
    return jnp.einsum('bhqk,bkhd->bqhd', a.astype(v.dtype), v)


def sb_core(q_args, q_pos, k, v, k_pos):
    (q,) = q_args
    z = jnp.einsum('bqhd,bkhd->bhqk', q, k).astype(F32) * (HEAD_DIM ** -0.5)
    mask = k_pos[None, :] < q_pos[:, None]
    log_1mb = jnp.where(mask, jax.nn.log_sigmoid(-z), 0.0)
    later = lax.cumsum(log_1mb, axis=3, reverse=True) - log_1mb
    A = jnp.where(mask, jnp.exp(jax.nn.log_sigmoid(z) + later), 0.0)
    return jnp.einsum('bhqk,bkhd->bqhd', A.astype(v.dtype), v)


def causal_dwconv(x, buf, w):
    B, T, C = x.shape
    if buf is None:
        buf = jnp.zeros((B, CONV_WIDTH - 1, C), x.dtype)
    xp = jnp.concatenate([buf.astype(x.dtype), x], axis=1)
    y = lax.conv_general_dilated(xp, w[:, None, :].astype(x.dtype), window_strides=(1,), padding='VALID',
                                 dimension_numbers=('NWC', 'WIO', 'NWC'), feature_group_count=C)
    return y, xp[:, -(CONV_WIDTH - 1):]


def gated_delta_chunked(q, k, v, g, beta, s0, chunk):
    B, T, H, dk = q.shape
    dv = v.shape[-1]
    n = T // chunk

    def blk(a):
        a = a.reshape((B, n, chunk, H) + a.shape[3:])
        return jnp.moveaxis(jnp.moveaxis(a, 1, 0), 3, 2)

    qc, kc, vc, gs, bc = blk(q), blk(k), blk(v), blk(g), blk(beta)
    gcum = jnp.cumsum(gs, axis=-1)
    idx = jnp.arange(chunk)
    incl = idx[:, None] >= idx[None, :]
    strict = idx[:, None] > idx[None, :]
    decay = jnp.exp(jnp.where(incl, gcum[..., :, None] - gcum[..., None, :], -jnp.inf))
    kb = kc * bc[..., None]
    m = jnp.where(strict, jnp.einsum('nbhid,nbhjd->nbhij', kb, kc) * decay, 0.0)
    rhs = jnp.concatenate([vc * bc[..., None], kb * jnp.exp(gcum)[..., None]], axis=-1)
    sol = lax.linalg.triangular_solve(m + jnp.eye(chunk, dtype=m.dtype), rhs, left_side=True, lower=True,
                                      unit_diagonal=True)
    u0, kcum = sol[..., :dv], sol[..., dv:]
    qk = jnp.einsum('nbhid,nbhjd->nbhij', qc, kc) * decay

    def step(S, xs):
        q_i, k_i, u_i, kc_i, g_i, a_i = xs
        v_new = u_i - jnp.einsum('bhcd,bhde->bhce', kc_i, S)
        o = (jnp.einsum('bhcd,bhde->bhce', q_i * jnp.exp(g_i)[..., None], S)
             + jnp.einsum('bhij,bhje->bhie', a_i, v_new))
        g_last = g_i[..., -1:]
        S = (S * jnp.exp(g_last)[..., None]
             + jnp.einsum('bhcd,bhce->bhde', k_i * jnp.exp(g_last - g_i)[..., None], v_new))
        return S, o

    S, o = lax.scan(step, s0, (qc, kc, u0, kcum, gcum, qk))
    o = jnp.moveaxis(jnp.moveaxis(o, 2, 3), 0, 1).reshape(B, T, H, dv)
    return o, S


def swiglu(h, wg, wu, wd):
    return (jax.nn.silu(h @ wg) * (h @ wu)) @ wd


def token_mixers(u, past, p, l):
    B, T, _ = u.shape
    P = 0 if past is None else past[0].shape[1]
    pf_k, pf_v, pf_lf, pd_k, pd_v, ps_k, ps_v, pg_s, pg_c = (None,) * 9 if past is None else past
    q_pos = P + jnp.arange(T)
    k_pos = jnp.arange(P + T)
    proj = u @ p['w_in'][l]
    fox, dif, sb, gdn, gates = jnp.split(proj, SPLITS, axis=-1)

    fw = FOX_HEADS * HEAD_DIM
    fq, fk, fv = (fox[..., i * fw:(i + 1) * fw].reshape(B, T, FOX_HEADS, HEAD_DIM) for i in range(3))
    logf = jax.nn.log_sigmoid(fox[..., 3 * fw:].astype(F32) + p['b_fox_f'][l].astype(F32))
    lf_all = _cat(pf_lf, logf)
    c = jnp.cumsum(lf_all, axis=1)
    o_fox = query_sweep(fox_core, (fq, c[:, P:]), q_pos, (_cat(pf_k, fk), _cat(pf_v, fv), c, k_pos))

    dw = DIFF_HEADS * 2 * DIFF_DK
    dq = dif[..., :dw].reshape(B, T, DIFF_HEADS, 2, DIFF_DK)
    dk = dif[..., dw:2 * dw].reshape(B, T, DIFF_HEADS, 2 * DIFF_DK)
    dv = dif[..., 2 * dw:].reshape(B, T, DIFF_HEADS, DIFF_DV)
    dk_all = _cat(pd_k, dk).reshape(B, P + T, DIFF_HEADS, 2, DIFF_DK)
    lam_init = 0.8 - 0.6 * math.exp(-0.3 * l)
    lam = (jnp.exp(jnp.sum(p['diff_lq1'][l].astype(F32) * p['diff_lk1'][l].astype(F32)))
           - jnp.exp(jnp.sum(p['diff_lq2'][l].astype(F32) * p['diff_lk2'][l].astype(F32))) + lam_init)
    o_diff = query_sweep(diff_core, (dq,), q_pos, (dk_all, _cat(pd_v, dv), k_pos, lam))
    o_diff = rms_norm(o_diff, p['diff_subln_g'][l]) * (1.0 - lam_init)

    sw = SB_HEADS * HEAD_DIM
    sq, sk, sv = (sb[..., i * sw:(i + 1) * sw].reshape(B, T, SB_HEADS, HEAD_DIM) for i in range(3))
    o_sb = query_sweep(sb_core, (sq,), q_pos, (_cat(ps_k, sk), _cat(ps_v, sv), k_pos))

    qkv, conv_state = causal_dwconv(gdn[..., :GDN_CONV_CH], pg_c, p['gdn_conv_w'][l])
    qkv = jax.nn.silu(qkv).astype(F32)
    kw = GDN_HEADS * GDN_DK
    gq = l2norm(qkv[..., :kw].reshape(B, T, GDN_HEADS, GDN_DK)) * (GDN_DK ** -0.5)
    gk = l2norm(qkv[..., kw:2 * kw].reshape(B, T, GDN_HEADS, GDN_DK))
    gv = qkv[..., 2 * kw:].reshape(B, T, GDN_HEADS, GDN_DV)
    a_logit = gdn[..., GDN_CONV_CH:GDN_CONV_CH + GDN_HEADS].astype(F32)
    b_logit = gdn[..., GDN_CONV_CH + GDN_HEADS:GDN_CONV_CH + 2 * GDN_HEADS].astype(F32)
    z = gdn[..., GDN_CONV_CH + 2 * GDN_HEADS:].reshape(B, T, GDN_HEADS, GDN_DV).astype(F32)
    g = -jnp.exp(p['gdn_a_log'][l].astype(F32)) * jax.nn.softplus(a_logit + p['gdn_dt_bias'][l].astype(F32))
    beta = jax.nn.sigmoid(b_logit)
    s0 = jnp.zeros((B, GDN_HEADS, GDN_DK, GDN_DV), F32) if pg_s is None else pg_s.astype(F32)
    o_g, s_new = gated_delta_chunked(gq, gk, gv, g, beta, s0, min(GDN_CHUNK, T))
    o_gdn = (rms_norm(o_g, p['gdn_norm_g'][l].astype(F32)) * jax.nn.silu(z)).astype(u.dtype)

    gate = jax.nn.sigmoid(gates.astype(F32)).astype(u.dtype).reshape(B, T, N_BRANCH, D_MODEL)
    wb = p['w_branch'][l]
    branches = (o_fox, o_diff, o_sb, o_gdn)
    merged = sum(gate[:, :, i] * (o.reshape(B, T, BRANCH_W).astype(u.dtype) @ wb[i]) for i, o in enumerate(branches))
    out = merged @ p['w_out'][l]
    new_state = (fk, fv, logf, dk, dv, sk, sv, s_new, conv_state)
    return out, new_state


def trunk(x, caches, p):
    per_layer = []
    for l in range(DEPTH):
        past = None if caches is None else tuple(c[l] for c in caches)
        mix, st = token_mixers(rms_norm(x, p['norm1_g'][l]), past, p, l)
        x = x + mix
        x = x + swiglu(rms_norm(x, p['norm2_g'][l]), p['w_ffn_gate'][l], p['w_ffn_up'][l], p['w_ffn_down'][l])
        per_layer.append(st)
    y = rms_norm(x, p['final_norm_g'])
    new_state = tuple(jnp.stack(parts) for parts in zip(*per_layer))
    return y, new_state


def setup_inputs(seed: int = 0) -> dict:
    key = jax.random.key(seed)
    ks = jax.random.split(key, 40)

    def nrm(i, shape, scale=1.0):
        return jax.random.normal(ks[i], shape, F32) * scale

    kv_fox = (DEPTH, DEC_BATCH, PAST_LEN, FOX_HEADS, HEAD_DIM)
    kv_sb = (DEPTH, DEC_BATCH, PAST_LEN, SB_HEADS, HEAD_DIM)
    dt = jnp.exp(jax.random.uniform(ks[30], (DEPTH, GDN_HEADS), F32, math.log(1e-3), math.log(1e-1)))
    return {
        'x_prompt': nrm(0, (BATCH, SEQ, D_MODEL)),
        'x_sample': nrm(1, (DEC_BATCH, DEC_SEQ, D_MODEL)),
        'cache_fox_k': nrm(2, kv_fox),
        'cache_fox_v': nrm(3, kv_fox),
        'cache_fox_logf': jax.nn.log_sigmoid(2.0 + nrm(4, (DEPTH, DEC_BATCH, PAST_LEN, FOX_HEADS))),
        'cache_diff_k': nrm(5, (DEPTH, DEC_BATCH, PAST_LEN, DIFF_HEADS, 2 * DIFF_DK)),
        'cache_diff_v': nrm(6, (DEPTH, DEC_BATCH, PAST_LEN, DIFF_HEADS, DIFF_DV)),
        'cache_sb_k': nrm(7, kv_sb),
        'cache_sb_v': nrm(8, kv_sb),
        'state_gdn': nrm(9, (DEPTH, DEC_BATCH, GDN_HEADS, GDN_DK, GDN_DV), 0.1),
        'state_gdn_conv': nrm(10, (DEPTH, DEC_BATCH, CONV_WIDTH - 1, GDN_CONV_CH)),
        'norm1_g': 1.0 + nrm(11, (DEPTH, D_MODEL), 0.02),
        'w_in': nrm(12, (DEPTH, D_MODEL, N_IN), D_MODEL ** -0.5),
        'b_fox_f': 2.0 + nrm(13, (DEPTH, FOX_HEADS), 0.1),
        'diff_lq1': nrm(14, (DEPTH, DIFF_DK), 0.1),
        'diff_lk1': nrm(15, (DEPTH, DIFF_DK), 0.1),
        'diff_lq2': nrm(16, (DEPTH, DIFF_DK), 0.1),
        'diff_lk2': nrm(17, (DEPTH, DIFF_DK), 0.1),
        'diff_subln_g': 1.0 + nrm(18, (DEPTH, DIFF_DV), 0.02),
        'gdn_conv_w': nrm(19, (DEPTH, CONV_WIDTH, GDN_CONV_CH), CONV_WIDTH ** -0.5),
        'gdn_a_log': jnp.log(jax.random.uniform(ks[20], (DEPTH, GDN_HEADS), F32, 1.0, 16.0)),
        'gdn_dt_bias': dt + jnp.log(-jnp.expm1(-dt)),
        'gdn_norm_g': 1.0 + nrm(21, (DEPTH, GDN_DV), 0.02),
        'w_branch': nrm(22, (DEPTH, N_BRANCH, BRANCH_W, D_MODEL), BRANCH_W ** -0.5),
        'w_out': nrm(23, (DEPTH, D_MODEL, D_MODEL), D_MODEL ** -0.5),
        'norm2_g': 1.0 + nrm(24, (DEPTH, D_MODEL), 0.02),
        'w_ffn_gate': nrm(25, (DEPTH, D_MODEL, D_FF), D_MODEL ** -0.5),
        'w_ffn_up': nrm(26, (DEPTH, D_MODEL, D_FF), D_MODEL ** -0.5),
        'w_ffn_down': nrm(27, (DEPTH, D_FF, D_MODEL), D_FF ** -0.5),
        'final_norm_g': 1.0 + nrm(28, (D_MODEL,), 0.02),
    }


def reference(x_prompt, x_sample, cache_fox_k, cache_fox_v, cache_fox_logf, cache_diff_k, cache_diff_v,
              cache_sb_k, cache_sb_v, state_gdn, state_gdn_conv, norm1_g, w_in, b_fox_f, diff_lq1, diff_lk1,
              diff_lq2, diff_lk2, diff_subln_g, gdn_conv_w, gdn_a_log, gdn_dt_bias, gdn_norm_g, w_branch, w_out,
              norm2_g, w_ffn_gate, w_ffn_up, w_ffn_down, final_norm_g):
    p = {'norm1_g': norm1_g, 'w_in': w_in, 'b_fox_f': b_fox_f, 'diff_lq1': diff_lq1, 'diff_lk1': diff_lk1,
         'diff_lq2': diff_lq2, 'diff_lk2': diff_lk2, 'diff_subln_g': diff_subln_g, 'gdn_conv_w': gdn_conv_w,
         'gdn_a_log': gdn_a_log, 'gdn_dt_bias': gdn_dt_bias, 'gdn_norm_g': gdn_norm_g, 'w_branch': w_branch,
         'w_out': w_out, 'norm2_g': norm2_g, 'w_ffn_gate': w_ffn_gate, 'w_ffn_up': w_ffn_up,
         'w_ffn_down': w_ffn_down, 'final_norm_g': final_norm_g}
    y_prompt, sp = trunk(x_prompt, None, p)
    caches = (cache_fox_k, cache_fox_v, cache_fox_logf, cache_diff_k, cache_diff_v, cache_sb_k, cache_sb_v,
              state_gdn, state_gdn_conv)
    y_sample, ss = trunk(x_sample, caches, p)
    (p_fox_k, p_fox_v, p_fox_logf, p_diff_k, p_diff_v, p_sb_k, p_sb_v, p_gdn_state, p_gdn_conv) = sp
    (s_fox_k, s_fox_v, s_fox_logf, s_diff_k, s_diff_v, s_sb_k, s_sb_v, s_gdn_state, s_gdn_conv) = ss
    return (y_prompt, y_sample,
            p_fox_k, p_fox_v, p_fox_logf, p_diff_k, p_diff_v, p_sb_k, p_sb_v, p_gdn_state, p_gdn_conv,
            s_fox_k, s_fox_v, s_fox_logf, s_diff_k, s_diff_v, s_sb_k, s_sb_v, s_gdn_state, s_gdn_conv)
```

```python
import functools
import math

import jax
import jax.numpy as jnp
from jax import lax
from jax.experimental import pallas as pl
from jax.experimental.pallas import tpu as pltpu

F32 = jnp.float32
BF16 = jnp.bfloat16

D_MODEL = 1024
DEPTH = 2
CHUNK = 64
HEAD_DIM = 64
N_HEADS = 4
DIFF_DK = 32
CONV_WIDTH = 4
BRANCH_W = 256
N_BRANCH = 4
D_FF = 2816
EPS = 1e-6
GDN_CONV_CH = 768

LANES = 128
NEG_BIG = -1e30
VMEM_LIMIT = 56 * 1024 * 1024

_FOX0, _DIF0, _SB0, _GDN0, _GATE0 = 0, 772, 1540, 2308, 3340
N_PROJ = 3456

HI = lax.Precision.HIGHEST


def _dot(a, b, precision=None):
    return jnp.dot(a, b, preferred_element_type=F32, precision=precision)


def _dot_nt(a, b, precision=None):
    return lax.dot_general(a, b, (((1,), (1,)), ((), ())), preferred_element_type=F32, precision=precision)


def _dot_tn(a, b, precision=None):
    return lax.dot_general(a, b, (((0,), (0,)), ((), ())), preferred_element_type=F32, precision=precision)


def _rms(x, g):
    return x * lax.rsqrt(jnp.mean(x * x, axis=-1, keepdims=True) + EPS) * g


def _softplus(t):
    return jnp.maximum(t, 0.0) + jnp.log1p(jnp.exp(-jnp.abs(t)))


def _sigmoid(t):
    return 1.0 / (1.0 + jnp.exp(-t))


def _const_spec(shape):
    nd = len(shape)
    return pl.BlockSpec(shape, lambda *_: (0,) * nd, pipeline_mode=pl.Buffered(1))


def _params(sem):
    return pltpu.CompilerParams(dimension_semantics=sem, vmem_limit_bytes=VMEM_LIMIT)


def _in_proj_kernel(x_ref, g_ref, w_ref, sp_ref,
                    fq_ref, fk32_ref, fk16_ref, fv32_ref, fv16_ref,
                    dq_ref, dk32_ref, dk16_ref, dv32_ref, dv16_ref,
                    sq_ref, sk32_ref, sk16_ref, sv32_ref, sv16_ref,
                    gqkv_ref, gz_ref, small_ref):
    u = _rms(x_ref[...], g_ref[...]).astype(BF16)

    def mm(c0, w):
        return _dot(u, w_ref[:, c0:c0 + w])

    def qkv(c0, q_scale, q_ref, k32, k16, v32, v16):
        q_ref[...] = (mm(c0, 256) * q_scale).astype(BF16)
        k = mm(c0 + 256, 256)
        k32[...] = k
        k16[...] = k.astype(BF16)
        v = mm(c0 + 512, 256)
        v32[...] = v
        v16[...] = v.astype(BF16)

    qkv(0, HEAD_DIM ** -0.5, fq_ref, fk32_ref, fk16_ref, fv32_ref, fv16_ref)
    qkv(768, DIFF_DK ** -0.5, dq_ref, dk32_ref, dk16_ref, dv32_ref, dv16_ref)
    qkv(1536, HEAD_DIM ** -0.5, sq_ref, sk32_ref, sk16_ref, sv32_ref, sv16_ref)
    gqkv_ref[...] = mm(2304, 768)
    gz_ref[...] = mm(3072, 256)
    t = mm(3328, 128) + sp_ref[0:1, :]
    lane = lax.broadcasted_iota(jnp.int32, (1, LANES), 1)
    sp = _softplus(t)
    logf = t - sp
    g = -jnp.exp(sp_ref[1:2, :]) * sp
    beta = _sigmoid(t)
    small_ref[...] = jnp.where(lane < 4, logf, jnp.where(lane < 8, g, jnp.where(lane < 12, beta, 0.0)))


def _in_proj(x2, g1, w_r, sp, tm):
    M = x2.shape[0]
    row = lambda w: pl.BlockSpec((tm, w), lambda i: (i, 0))
    f32o = lambda w: jax.ShapeDtypeStruct((M, w), F32)
    b16o = lambda w: jax.ShapeDtypeStruct((M, w), BF16)
    qkv_shapes = [b16o(256), f32o(256), b16o(256), f32o(256), b16o(256)]
    out_shape = qkv_shapes * 3 + [f32o(768), f32o(256), f32o(128)]
    out_specs = [row(256)] * 15 + [row(768), row(256), row(128)]
    return pl.pallas_call(
        _in_proj_kernel,
        grid=(M // tm,),
        in_specs=[row(D_MODEL), _const_spec((1, D_MODEL)), _const_spec((D_MODEL, N_PROJ)), _const_spec((8, LANES))],
        out_specs=out_specs,
        out_shape=out_shape,
        compiler_params=_params(("parallel",)),
        name="in_proj",
    )(x2, g1, w_r, sp)


def _cumsum_kernel(x_ref, o_ref):
    R = x_ref.shape[2]
    ii = lax.broadcasted_iota(jnp.int32, (LANES, LANES), 0)
    jj = lax.broadcasted_iota(jnp.int32, (LANES, LANES), 1)
    upper = (ii <= jj).astype(F32)
    ri = lax.broadcasted_iota(jnp.int32, (R, R), 0)
    rj = lax.broadcasted_iota(jnp.int32, (R, R), 1)
    lower = (rj < ri).astype(F32)
    for h in range(N_HEADS):
        within = _dot(x_ref[0, h], upper, HI)
        tot = jnp.broadcast_to(within[:, LANES - 1:LANES], (R, LANES))
        o_ref[0, h] = within + _dot(lower, tot, HI)


def _cumsum_time(x4):
    B, H, R, _ = x4.shape
    spec = pl.BlockSpec((1, H, R, LANES), lambda b: (b, 0, 0, 0))
    return pl.pallas_call(
        _cumsum_kernel, grid=(B,), in_specs=[spec], out_specs=spec,
        out_shape=jax.ShapeDtypeStruct(x4.shape, F32),
        compiler_params=_params(("parallel",)), name="fox_cumsum",
    )(x4)


def _lane_group(q, lo, hi):
    lane = lax.broadcasted_iota(jnp.int32, (1, LANES), 1)
    return jnp.where((lane >= lo) & (lane < hi), q, jnp.zeros_like(q))


def _positions(qi, n_full, tq, tk, past):
    q_pos = past + qi * tq + lax.broadcasted_iota(jnp.int32, (tq, 1), 0)
    k_pos = n_full * tk + lax.broadcasted_iota(jnp.int32, (1, tk), 1)
    return q_pos, k_pos


def _kv_chunk(k_ref, v_ref, j, tk):
    r0 = pl.multiple_of(j * tk, tk)
    return k_ref[0, pl.ds(r0, tk), :], v_ref[0, pl.ds(r0, tk), :]


def _softmax_step(s, v, m, l, acc):
    m_new = jnp.maximum(m, jnp.max(s, axis=1, keepdims=True))
    alpha = jnp.exp(m - m_new)
    p = jnp.exp(s - m_new)
    l = alpha * l + jnp.sum(p, axis=1, keepdims=True)
    acc = alpha * acc + _dot(p.astype(BF16), v)
    return m_new, l, acc


def _softmax_init(tq):
    return (jnp.full((tq, 1), NEG_BIG, F32), jnp.zeros((tq, 1), F32), jnp.zeros((tq, LANES), F32))


def _fox_kernel(q_ref, k_ref, v_ref, cq_ref, ck_ref, o_ref, *, tq, tk, past):
    qi = pl.program_id(2)
    n_full = (past + qi * tq) // tk
    q = q_ref[0]
    qh = [_lane_group(q, 0, 64), _lane_group(q, 64, 128)]
    cq = [cq_ref[0, 0], cq_ref[0, 1]]
    q_pos, k_pos = _positions(qi, n_full, tq, tk, past)

    def step(j, carry, masked):
        k, v = _kv_chunk(k_ref, v_ref, j, tk)
        out = []
        for hh in range(2):
            s = _dot_nt(qh[hh], k) + cq[hh] - ck_ref[0, hh, pl.ds(j, 1), :]
            if masked:
                s = jnp.where(k_pos <= q_pos, s, NEG_BIG)
            out.append(_softmax_step(s, v, *carry[hh]))
        return tuple(out)

    carry = lax.fori_loop(0, n_full, lambda j, c: step(j, c, False), (_softmax_init(tq),) * 2)
    (_, l0, a0), (_, l1, a1) = step(n_full, carry, True)
    lane = lax.broadcasted_iota(jnp.int32, (1, LANES), 1)
    o_ref[0] = jnp.where(lane < 64, a0 / l0, a1 / l1).astype(o_ref.dtype)


def _diff_kernel(dl_ref, g_ref, q_ref, k_ref, v_ref, o_ref, *, tq, tk, past, lam_init):
    qi = pl.program_id(2)
    n_full = (past + qi * tq) // tk
    q = q_ref[0]
    qg = [_lane_group(q, DIFF_DK * g, DIFF_DK * (g + 1)) for g in range(4)]
    q_pos, k_pos = _positions(qi, n_full, tq, tk, past)

    def step(j, carry, masked):
        k, v = _kv_chunk(k_ref, v_ref, j, tk)
        out = []
        for g in range(4):
            s = _dot_nt(qg[g], k)
            if masked:
                s = jnp.where(k_pos // CHUNK <= q_pos // CHUNK, s, NEG_BIG)
            out.append(_softmax_step(s, v, *carry[g]))
        return tuple(out)

    carry = lax.fori_loop(0, n_full, lambda j, c: step(j, c, False), (_softmax_init(tq),) * 4)
    res = step(n_full, carry, True)
    dl = dl_ref[...]
    lam = (jnp.exp(jnp.sum(dl[0:1] * dl[1:2], axis=1, keepdims=True))
           - jnp.exp(jnp.sum(dl[2:3] * dl[3:4], axis=1, keepdims=True)) + lam_init)
    p = [a / l for (_, l, a) in res]
    lane = lax.broadcasted_iota(jnp.int32, (1, LANES), 1)
    first = lane < 64
    o = jnp.where(first, p[0] - lam * p[1], p[2] - lam * p[3])
    sq = o * o
    ms = jnp.where(first,
                   jnp.sum(jnp.where(first, sq, 0.0), axis=1, keepdims=True),
                   jnp.sum(jnp.where(first, 0.0, sq), axis=1, keepdims=True)) * (1.0 / HEAD_DIM)
    o_ref[0] = (o * lax.rsqrt(ms + EPS) * g_ref[...] * (1.0 - lam_init)).astype(o_ref.dtype)


def _sb_kernel(q_ref, k_ref, v_ref, o_ref, *, tq, tk, past):
    qi = pl.program_id(2)
    n_full = (past + qi * tq) // tk
    q = q_ref[0]
    qh = [_lane_group(q, 0, 64), _lane_group(q, 64, 128)]
    q_pos, k_pos = _positions(qi, n_full, tq, tk, past)
    ki = lax.broadcasted_iota(jnp.int32, (tk, tk), 0)
    kj = lax.broadcasted_iota(jnp.int32, (tk, tk), 1)
    after = (ki > kj).astype(BF16)

    def step(j, carry, masked):
        k, v = _kv_chunk(k_ref, v_ref, j, tk)
        out = []
        for hh in range(2):
            right, acc = carry[hh]
            z = _dot_nt(qh[hh], k)
            sp = jnp.log1p(jnp.exp(-jnp.abs(z)))
            log_1mb = -(jnp.maximum(z, 0.0) + sp)
            log_b = jnp.minimum(z, 0.0) - sp
            if masked:
                mask = k_pos < q_pos
                log_1mb = jnp.where(mask, log_1mb, 0.0)
            hi = log_1mb.astype(BF16)
            lo = (log_1mb - hi.astype(F32)).astype(BF16)
            later = _dot(hi, after) + _dot(lo, after)
            a = jnp.exp(log_b + later + right)
            if masked:
                a = jnp.where(mask, a, 0.0)
            acc = acc + _dot(a.astype(BF16), v)
            right = right + later[:, 0:1] + log_1mb[:, 0:1]
            out.append((right, acc))
        return tuple(out)

    init = (jnp.zeros((tq, 1), F32), jnp.zeros((tq, LANES), F32))
    carry = step(n_full, (init, init), True)
    carry = lax.fori_loop(0, n_full, lambda i, c: step(n_full - 1 - i, c, False), carry)
    lane = lax.broadcasted_iota(jnp.int32, (1, LANES), 1)
    o_ref[0] = jnp.where(lane < 64, carry[0][1], carry[1][1]).astype(o_ref.dtype)


def _attention(body, q, k, v, extra, extra_specs, *, tq, tk, past, name, **kw):
    B, Tq, _ = q.shape
    Tk = k.shape[1]
    assert Tq % tq == 0 and tq % CHUNK == 0 and past % tk == 0
    assert tq == tk or (tq < tk and Tq == tq)
    assert past + Tq <= Tk and Tk % tk == 0
    blk = pl.BlockSpec((1, tq, LANES), lambda b, p, i: (b, i, p))
    kv = pl.BlockSpec((1, Tk, LANES), lambda b, p, i: (b, 0, p))
    return pl.pallas_call(
        functools.partial(body, tq=tq, tk=tk, past=past, **kw),
        grid=(B, 2, Tq // tq),
        in_specs=list(extra_specs) + [blk, kv, kv],
        out_specs=blk,
        out_shape=jax.ShapeDtypeStruct((B, Tq, 2 * LANES), BF16),
        compiler_params=_params(("parallel", "parallel", "arbitrary")),
        name=name,
    )(*extra, q, k, v)


def _fox_attention(q, k, v, cq, ck, **kw):
    tq, nck, tk = kw["tq"], ck.shape[2], kw["tk"]
    B, Tq, _ = q.shape
    Tk = k.shape[1]
    blk = pl.BlockSpec((1, tq, LANES), lambda b, p, i: (b, i, p))
    kv = pl.BlockSpec((1, Tk, LANES), lambda b, p, i: (b, 0, p))
    cq_spec = pl.BlockSpec((1, 2, tq, 1), lambda b, p, i: (b, p, i, 0))
    ck_spec = pl.BlockSpec((1, 2, nck, tk), lambda b, p, i: (b, p, 0, 0))
    assert Tq % tq == 0 and kw["past"] % tk == 0 and (tq == tk or (tq < tk and Tq == tq))
    return pl.pallas_call(
        functools.partial(_fox_kernel, **kw),
        grid=(B, 2, Tq // tq),
        in_specs=[blk, kv, kv, cq_spec, ck_spec],
        out_specs=blk,
        out_shape=jax.ShapeDtypeStruct((B, Tq, 2 * LANES), BF16),
        compiler_params=_params(("parallel", "parallel", "arbitrary")),
        name="fox_attention",
    )(q, k, v, cq, ck)


def _gdn_kernel(x_ref, z_ref, sm_ref, cw_ref, buf_ref, s0_ref, ng_ref,
                o_ref, sout_ref, cout_ref, xs_ref, qkv_ref, st_ref, *, tb):
    t = pl.program_id(1)
    C = CHUNK
    tail = CONV_WIDTH - 1

    @pl.when(t == 0)
    def _():
        st_ref[...] = s0_ref[0]
        xs_ref[0:8, :] = jnp.zeros((8, GDN_CONV_CH), F32)
        xs_ref[8 - tail:8, :] = buf_ref[0]

    @pl.when(t > 0)
    def _():
        xs_ref[0:8, :] = xs_ref[tb:tb + 8, :]

    xs_ref[8:, :] = x_ref[0]
    y = cw_ref[0:1, :] * xs_ref[8 - tail:8 - tail + tb, :]
    for i in range(1, CONV_WIDTH):
        y = y + cw_ref[i:i + 1, :] * xs_ref[8 - tail + i:8 - tail + i + tb, :]
    y = y * _sigmoid(y)
    gi = lax.broadcasted_iota(jnp.int32, (256, 256), 0) // HEAD_DIM
    gj = lax.broadcasted_iota(jnp.int32, (256, 256), 1) // HEAD_DIM
    same_head = (gi == gj).astype(F32)
    q = y[:, 0:256]
    k = y[:, 256:512]
    qkv_ref[:, 0:256] = q * lax.rsqrt(_dot(q * q, same_head, HI) + EPS) * (HEAD_DIM ** -0.5)
    qkv_ref[:, 256:512] = k * lax.rsqrt(_dot(k * k, same_head, HI) + EPS)
    qkv_ref[:, 512:768] = y[:, 512:768]

    ci = lax.broadcasted_iota(jnp.int32, (C, C), 0)
    cj = lax.broadcasted_iota(jnp.int32, (C, C), 1)
    incl = ci >= cj
    strict = ci > cj
    incl_f = incl.astype(F32)
    upto_f = (ci <= cj).astype(F32)
    ones = jnp.ones((C, C), F32)
    eye = (ci == cj).astype(F32)

    def chunk(c, states):
        r0 = pl.multiple_of(c * C, C)
        rows = pl.ds(r0, C)
        sm = sm_ref[0, rows, :]
        gcum_all = _dot(incl_f, sm, HI)
        new_states = []
        outs = []
        for h in range(N_HEADS):
            cols = slice(HEAD_DIM * h, HEAD_DIM * (h + 1))
            qh = qkv_ref[rows, HEAD_DIM * h:HEAD_DIM * (h + 1)]
            kh = qkv_ref[rows, 256 + HEAD_DIM * h:256 + HEAD_DIM * (h + 1)]
            vh = qkv_ref[rows, 512 + HEAD_DIM * h:512 + HEAD_DIM * (h + 1)]
            g_col = sm[:, 4 + h:5 + h]
            beta = sm[:, 8 + h:9 + h]
            gcum = gcum_all[:, 4 + h:5 + h]
            gcum_row = _dot(ones, jnp.broadcast_to(g_col, (C, C)) * upto_f, HI)
            decay = jnp.where(incl, jnp.exp(jnp.where(incl, gcum - gcum_row, 0.0)), 0.0)
            kb = kh * beta
            m = jnp.where(strict, _dot_nt(kb, kh, HI) * decay, 0.0)
            pw = -m
            inv = eye + pw
            for _ in range(5):
                pw = _dot(pw, pw, HI)
                inv = inv + _dot(inv, pw, HI)
            eg = jnp.exp(gcum)
            u0 = _dot(inv, vh * beta, HI)
            kcum = _dot(inv, kb * eg, HI)
            qk = jnp.where(incl, _dot_nt(qh, kh, HI) * decay, 0.0)
            S = states[h]
            v_new = u0 - _dot(kcum, S, HI)
            o = _dot(qh * eg, S, HI) + _dot(qk, v_new, HI)
            g_last = gcum[C - 1:C, :]
            S = S * jnp.exp(g_last) + _dot_tn(kh * jnp.exp(g_last - gcum), v_new, HI)
            new_states.append(S)
            zh = z_ref[0, rows, cols]
            og = _rms(o, ng_ref[...]) * (zh * _sigmoid(zh))
            outs.append(og)
        o_ref[0, rows, :] = jnp.concatenate(outs, axis=1).astype(o_ref.dtype)
        return tuple(new_states)

    states = lax.fori_loop(0, tb // C, chunk, tuple(st_ref[h] for h in range(N_HEADS)))
    for h in range(N_HEADS):
        st_ref[h] = states[h]

    @pl.when(t == pl.num_programs(1) - 1)
    def _():
        for h in range(N_HEADS):
            sout_ref[0, h] = states[h]
        cout_ref[0] = xs_ref[8 + tb - tail:8 + tb, :]


def _gdn(x, z, small, conv_w, buf, s0, norm_g, tb):
    B, T, _ = x.shape
    assert T % tb == 0 and tb % CHUNK == 0 and tb >= 8
    tok = lambda w: pl.BlockSpec((1, tb, w), lambda b, t: (b, t, 0))
    per_b = lambda s: pl.BlockSpec((1,) + s, lambda b, t: (b,) + (0,) * len(s))
    return pl.pallas_call(
        functools.partial(_gdn_kernel, tb=tb),
        grid=(B, T // tb),
        in_specs=[tok(768), tok(256), tok(128),
                  pl.BlockSpec((CONV_WIDTH, 768), lambda b, t: (0, 0)),
                  per_b((CONV_WIDTH - 1, 768)), per_b((N_HEADS, 64, 64)),
                  pl.BlockSpec((1, 64), lambda b, t: (0, 0))],
        out_specs=[tok(256), per_b((N_HEADS, 64, 64)), per_b((CONV_WIDTH - 1, 768))],
        out_shape=[jax.ShapeDtypeStruct((B, T, 256), BF16),
                   jax.ShapeDtypeStruct((B, N_HEADS, 64, 64), F32),
                   jax.ShapeDtypeStruct((B, CONV_WIDTH - 1, 768), F32)],
        scratch_shapes=[pltpu.VMEM((tb + 8, 768), F32), pltpu.VMEM((tb, 768), F32),
                        pltpu.VMEM((N_HEADS, 64, 64), F32)],
        compiler_params=_params(("parallel", "arbitrary")),
        name="gated_deltanet",
    )(x, z, small, conv_w, buf, s0, norm_g)


def _merge_kernel(x_ref, g1_ref, g2_ref, of_ref, od_ref, os_ref, og_ref, wgate_ref, wb_ref, wo_ref,
                  x1_ref, h_ref):
    x = x_ref[...]
    u = _rms(x, g1_ref[...]).astype(BF16)
    merged = None
    for i, o_ref in enumerate((of_ref, od_ref, os_ref, og_ref)):
        gate = _sigmoid(_dot(u, wgate_ref[:, i * D_MODEL:(i + 1) * D_MODEL]))
        term = gate * _dot(o_ref[...], wb_ref[i])
        merged = term if merged is None else merged + term
    x1 = x + _dot(merged.astype(BF16), wo_ref[...])
    x1_ref[...] = x1
    h_ref[...] = _rms(x1, g2_ref[...]).astype(BF16)


def _merge(x2, g1, g2, o_fox, o_diff, o_sb, o_gdn, w_gate, w_branch, w_out, tm):
    M = x2.shape[0]
    row = lambda w: pl.BlockSpec((tm, w), lambda i: (i, 0))
    return pl.pallas_call(
        _merge_kernel,
        grid=(M // tm,),
        in_specs=[row(D_MODEL), _const_spec((1, D_MODEL)), _const_spec((1, D_MODEL)),
                  row(256), row(256), row(256), row(256),
                  _const_spec((D_MODEL, N_BRANCH * D_MODEL)), _const_spec((N_BRANCH, BRANCH_W, D_MODEL)),
                  _const_spec((D_MODEL, D_MODEL))],
        out_specs=[row(D_MODEL), row(D_MODEL)],
        out_shape=[jax.ShapeDtypeStruct((M, D_MODEL), F32), jax.ShapeDtypeStruct((M, D_MODEL), BF16)],
        compiler_params=_params(("parallel",)),
        name="branch_merge",
    )(x2, g1, g2, o_fox, o_diff, o_sb, o_gdn, w_gate, w_branch, w_out)


def _ffn_kernel(x1_ref, h_ref, wg_ref, wu_ref, wd_ref, gf_ref, o_ref, *, final):
    h = h_ref[...]
    acc = x1_ref[...]
    half = D_FF // 2
    for f0 in (0, half):
        a = _dot(h, wg_ref[:, f0:f0 + half])
        b = _dot(h, wu_ref[:, f0:f0 + half])
        acc = acc + _dot((a * _sigmoid(a) * b).astype(BF16), wd_ref[f0:f0 + half, :])
    if final:
        acc = _rms(acc, gf_ref[...])
    o_ref[...] = acc


def _ffn(x1, h, wg, wu, wd, g_final, final, tm):
    M = x1.shape[0]
    row = pl.BlockSpec((tm, D_MODEL), lambda i: (i, 0))
    return pl.pallas_call(
        functools.partial(_ffn_kernel, final=final),
        grid=(M // tm,),
        in_specs=[row, row, _const_spec((D_MODEL, D_FF)), _const_spec((D_MODEL, D_FF)),
                  _const_spec((D_FF, D_MODEL)), _const_spec((1, D_MODEL))],
        out_specs=row,
        out_shape=jax.ShapeDtypeStruct((M, D_MODEL), F32),
        compiler_params=_params(("parallel",)),
        name="swiglu",
    )(x1, h, wg, wu, wd, g_final)


def _prep_layer(p, l):
    w = p["w_in"][l]
    cols = lambda a, b: w[:, a:b]
    small = jnp.concatenate([cols(_FOX0 + 768, _FOX0 + 772), cols(_GDN0 + 768, _GDN0 + 776),
                             jnp.zeros((D_MODEL, LANES - 12), F32)], axis=1)
    w_r = jnp.concatenate([cols(_FOX0, _FOX0 + 768), cols(_DIF0, _DIF0 + 768), cols(_SB0, _SB0 + 768),
                           cols(_GDN0, _GDN0 + 768), cols(_GDN0 + 776, _GDN0 + 1032), small], axis=1).astype(BF16)
    sp = jnp.zeros((8, LANES), F32)
    sp = sp.at[0, 0:4].set(p["b_fox_f"][l]).at[0, 4:8].set(p["gdn_dt_bias"][l]).at[1, 4:8].set(p["gdn_a_log"][l])
    return dict(
        w_r=w_r, sp=sp, w_gate=cols(_GATE0, _GATE0 + N_BRANCH * D_MODEL).astype(BF16),
        g1=p["norm1_g"][l].reshape(1, D_MODEL), g2=p["norm2_g"][l].reshape(1, D_MODEL),
        dl=jnp.stack([p["diff_lq1"][l], p["diff_lk1"][l], p["diff_lq2"][l], p["diff_lk2"][l]]),
        subln=jnp.tile(p["diff_subln_g"][l], 2).reshape(1, LANES),
        conv_w=p["gdn_conv_w"][l], norm_g=p["gdn_norm_g"][l].reshape(1, 64),
        w_branch=p["w_branch"][l].astype(BF16), w_out=p["w_out"][l].astype(BF16),
        wg=p["w_ffn_gate"][l].astype(BF16), wu=p["w_ffn_up"][l].astype(BF16), wd=p["w_ffn_down"][l].astype(BF16),
    )


def _with_past(past, new16, t_pad):
    B, T, W = new16.shape
    parts = [new16] if past is None else [past.reshape(B, -1, W).astype(BF16), new16]
    n = sum(a.shape[1] for a in parts)
    if t_pad > n:
        parts.append(jnp.zeros((B, t_pad - n, W), BF16))
    return jnp.concatenate(parts, axis=1) if len(parts) > 1 else parts[0]


def _trunk(x, caches, p, layers, *, tm, tq, tk, tb):
    B, T, _ = x.shape
    M = B * T
    past = 0 if caches is None else caches[0].shape[2]
    t_keys = -(-(past + T) // tk) * tk
    x2 = x.reshape(M, D_MODEL)
    new_state = []
    for l in range(DEPTH):
        lp = layers[l]
        c = None if caches is None else tuple(a[l] for a in caches)
        (fq, fk32, fk16, fv32, fv16, dq, dk32, dk16, dv32, dv16, sq, sk32, sk16, sv32, sv16,
         gqkv, gz, small) = _in_proj(x2, lp["g1"], lp["w_r"], lp["sp"], tm)
        b3 = lambda a: a.reshape(B, T, a.shape[-1])
        kv = lambda i, a: _with_past(None if c is None else c[i], b3(a), t_keys)

        logf = b3(small)[:, :, 0:N_HEADS]
        lf_all = logf if c is None else jnp.concatenate([c[2], logf], axis=1)
        r_pad = -(-t_keys // (8 * LANES)) * 8
        lf_all = jnp.pad(lf_all, ((0, 0), (0, r_pad * LANES - lf_all.shape[1]), (0, 0)))
        cum = _cumsum_time(jnp.swapaxes(lf_all, 1, 2).reshape(B, N_HEADS, r_pad, LANES)).reshape(B, N_HEADS, -1)
        cq = cum[:, :, past:past + T].reshape(B, N_HEADS, T, 1)
        ck = cum[:, :, :t_keys].reshape(B, N_HEADS, t_keys // tk, tk)
        o_fox = _fox_attention(b3(fq), kv(0, fk16), kv(1, fv16), cq, ck, tq=tq, tk=tk, past=past)

        lam_init = 0.8 - 0.6 * math.exp(-0.3 * l)
        o_diff = _attention(_diff_kernel, b3(dq), kv(3, dk16), kv(4, dv16), (lp["dl"], lp["subln"]),
                            (pl.BlockSpec((4, DIFF_DK), lambda b, p_, i: (0, 0)),
                             pl.BlockSpec((1, LANES), lambda b, p_, i: (0, 0))),
                            tq=tq, tk=tk, past=past, name="diff_attention", lam_init=lam_init)

        o_sb = _attention(_sb_kernel, b3(sq), kv(5, sk16), kv(6, sv16), (), (),
                          tq=tq, tk=tk, past=past, name="sb_attention")

        buf = jnp.zeros((B, CONV_WIDTH - 1, GDN_CONV_CH), F32) if c is None else c[8]
        s0 = jnp.zeros((B, N_HEADS, 64, 64), F32) if c is None else c[7]
        o_gdn, s_new, conv_state = _gdn(b3(gqkv), b3(gz), b3(small), lp["conv_w"], buf, s0, lp["norm_g"], tb)

        x1, h = _merge(x2, lp["g1"], lp["g2"], o_fox.reshape(M, 256), o_diff.reshape(M, 256),
                       o_sb.reshape(M, 256), o_gdn.reshape(M, 256), lp["w_gate"], lp["w_branch"], lp["w_out"], tm)
        x2 = _ffn(x1, h, lp["wg"], lp["wu"], lp["wd"], p["final_norm_g"].reshape(1, D_MODEL),
                  l == DEPTH - 1, tm)
        h4 = lambda a: a.reshape(B, T, N_HEADS, HEAD_DIM)
        new_state.append((h4(fk32), h4(fv32), logf, h4(dk32), h4(dv32), h4(sk32), h4(sv32), s_new, conv_state))
    y = x2.reshape(B, T, D_MODEL)
    return y, tuple(jnp.stack(parts) for parts in zip(*new_state))


def kernel(x_prompt, x_sample, cache_fox_k, cache_fox_v, cache_fox_logf, cache_diff_k, cache_diff_v, cache_sb_k, cache_sb_v, state_gdn, state_gdn_conv, norm1_g, w_in, b_fox_f, diff_lq1, diff_lk1, diff_lq2, diff_lk2, diff_subln_g, gdn_conv_w, gdn_a_log, gdn_dt_bias, gdn_norm_g, w_branch, w_out, norm2_g, w_ffn_gate, w_ffn_up, w_ffn_down, final_norm_g):
    p = dict(norm1_g=norm1_g, w_in=w_in, b_fox_f=b_fox_f, diff_lq1=diff_lq1, diff_lk1=diff_lk1, diff_lq2=diff_lq2,
             diff_lk2=diff_lk2, diff_subln_g=diff_subln_g, gdn_conv_w=gdn_conv_w, gdn_a_log=gdn_a_log,
             gdn_dt_bias=gdn_dt_bias, gdn_norm_g=gdn_norm_g, w_branch=w_branch, w_out=w_out, norm2_g=norm2_g,
             w_ffn_gate=w_ffn_gate, w_ffn_up=w_ffn_up, w_ffn_down=w_ffn_down, final_norm_g=final_norm_g)
    layers = [_prep_layer(p, l) for l in range(DEPTH)]
    t_p = x_prompt.shape[1]
    y_prompt, sp = _trunk(x_prompt, None, p, layers, tm=min(512, t_p), tq=min(256, t_p), tk=min(256, t_p),
                          tb=min(512, t_p))
    caches = (cache_fox_k, cache_fox_v, cache_fox_logf, cache_diff_k, cache_diff_v, cache_sb_k, cache_sb_v,
              state_gdn, state_gdn_conv)
    t_s = x_sample.shape[1]
    y_sample, ss = _trunk(x_sample, caches, p, layers, tm=min(512, x_sample.shape[0] * t_s), tq=t_s, tk=2 * t_s,
                          tb=t_s)
    return (y_prompt, y_sample) + sp + ss
```

```python
import functools
import math

import jax
import jax.numpy as jnp
from jax import lax
from jax.experimental import pallas as pl
from jax.experimental.pallas import tpu as pltpu

F32 = jnp.float32
BF16 = jnp.bfloat16

D_MODEL = 1024
DEPTH = 2
CHUNK = 64
HEAD_DIM = 64
N_HEADS = 4
DIFF_DK = 32
CONV_WIDTH = 4
BRANCH_W = 256
N_BRANCH = 4
D_FF = 2816
EPS = 1e-6
GDN_CONV_CH = 768

LANES = 128
NEG_BIG = -1e30
VMEM_LIMIT = 56 * 1024 * 1024
LOG2E = 1.4426950408889634

_FOX0, _DIF0, _SB0, _GDN0, _GATE0 = 0, 772, 1540, 2308, 3340
N_PROJ = 3456

_NN = (((1,), (0,)), ((), ()))
_NT = (((1,), (1,)), ((), ()))
_TN = (((0,), (0,)), ((), ()))


def _dot(a, b):
    return jnp.dot(a, b, preferred_element_type=F32)


def _dot_nt(a, b):
    return lax.dot_general(a, b, _NT, preferred_element_type=F32)


def _split2(a):
    hi = a.astype(BF16)
    return hi, (a - hi.astype(F32)).astype(BF16)


def _dot3(a, b, dims=_NN):
    ah, al = _split2(a)
    bh, bl = _split2(b)
    dg = lambda x, y: lax.dot_general(x, y, dims, preferred_element_type=F32)
    return dg(ah, bh) + (dg(ah, bl) + dg(al, bh))


def _dot_sel(x, sel, x_first=True):
    hi = x.astype(BF16)
    r = x - hi.astype(F32)
    mid = r.astype(BF16)
    lo = (r - mid.astype(F32)).astype(BF16)
    if x_first:
        return _dot(hi, sel) + (_dot(mid, sel) + _dot(lo, sel))
    return _dot(sel, hi) + (_dot(sel, mid) + _dot(sel, lo))


def _rms(x, g):
    return x * lax.rsqrt(jnp.mean(x * x, axis=-1, keepdims=True) + EPS) * g


def _softplus(t):
    return jnp.maximum(t, 0.0) + jnp.log1p(jnp.exp(-jnp.abs(t)))


def _sigmoid(t):
    return 1.0 / (1.0 + jnp.exp(-t))


def _const_spec(shape):
    nd = len(shape)
    return pl.BlockSpec(shape, lambda *_: (0,) * nd, pipeline_mode=pl.Buffered(1))


def _params(sem):
    return pltpu.CompilerParams(dimension_semantics=sem, vmem_limit_bytes=VMEM_LIMIT)


def _in_proj_kernel(x_ref, g_ref, w_ref, sp_ref,
                    fq_ref, fk32_ref, fk16_ref, fv32_ref, fv16_ref,
                    dq_ref, dk32_ref, dk16_ref, dv32_ref, dv16_ref,
                    sq_ref, sk32_ref, sk16_ref, sv32_ref, sv16_ref,
                    gqkv_ref, gz_ref, small_ref):
    u = _rms(x_ref[...], g_ref[...]).astype(BF16)

    def mm(c0, w):
        return _dot(u, w_ref[:, c0:c0 + w])

    def qkv(c0, q_scale, q_ref, k32, k16, v32, v16):
        q_ref[...] = (mm(c0, 256) * q_scale).astype(BF16)
        k = mm(c0 + 256, 256)
        k32[...] = k
        k16[...] = k.astype(BF16)
        v = mm(c0 + 512, 256)
        v32[...] = v
        v16[...] = v.astype(BF16)

    qkv(0, HEAD_DIM ** -0.5 * LOG2E, fq_ref, fk32_ref, fk16_ref, fv32_ref, fv16_ref)
    qkv(768, DIFF_DK ** -0.5 * LOG2E, dq_ref, dk32_ref, dk16_ref, dv32_ref, dv16_ref)
    qkv(1536, HEAD_DIM ** -0.5, sq_ref, sk32_ref, sk16_ref, sv32_ref, sv16_ref)
    gqkv_ref[...] = mm(2304, 768)
    gz_ref[...] = mm(3072, 256)
    t = mm(3328, 128) + sp_ref[0:1, :]
    lane = lax.broadcasted_iota(jnp.int32, (1, LANES), 1)
    sp = _softplus(t)
    logf = t - sp
    g = -jnp.exp(sp_ref[1:2, :]) * sp
    beta = _sigmoid(t)
    small_ref[...] = jnp.where(lane < 4, logf, jnp.where(lane < 8, g, jnp.where(lane < 12, beta, 0.0)))


def _in_proj(x2, g1, w_r, sp, tm):
    M = x2.shape[0]
    row = lambda w: pl.BlockSpec((tm, w), lambda i: (i, 0))
    f32o = lambda w: jax.ShapeDtypeStruct((M, w), F32)
    b16o = lambda w: jax.ShapeDtypeStruct((M, w), BF16)
    qkv_shapes = [b16o(256), f32o(256), b16o(256), f32o(256), b16o(256)]
    out_shape = qkv_shapes * 3 + [f32o(768), f32o(256), f32o(128)]
    out_specs = [row(256)] * 15 + [row(768), row(256), row(128)]
    return pl.pallas_call(
        _in_proj_kernel,
        grid=(M // tm,),
        in_specs=[row(D_MODEL), _const_spec((1, D_MODEL)), _const_spec((D_MODEL, N_PROJ)), _const_spec((8, LANES))],
        out_specs=out_specs,
        out_shape=out_shape,
        compiler_params=_params(("parallel",)),
        name="in_proj",
    )(x2, g1, w_r, sp)


def _cumsum_kernel(x_ref, hi_ref, mid_ref, lo_ref):
    R = x_ref.shape[2]
    ii = lax.broadcasted_iota(jnp.int32, (LANES, LANES), 0)
    jj = lax.broadcasted_iota(jnp.int32, (LANES, LANES), 1)
    upper = (ii <= jj).astype(BF16)
    ri = lax.broadcasted_iota(jnp.int32, (R, R), 0)
    rj = lax.broadcasted_iota(jnp.int32, (R, R), 1)
    lower = (rj < ri).astype(BF16)
    for h in range(N_HEADS):
        within = _dot_sel(x_ref[0, h], upper)
        tot = jnp.broadcast_to(within[:, LANES - 1:LANES], (R, LANES))
        c = (within + _dot_sel(tot, lower, x_first=False)) * LOG2E
        hi = c.astype(BF16)
        r = c - hi.astype(F32)
        mid = r.astype(BF16)
        hi_ref[0, h] = hi
        mid_ref[0, h] = mid
        lo_ref[0, h] = (r - mid.astype(F32)).astype(BF16)


def _cumsum_time(x4):
    B, H, R, _ = x4.shape
    spec = pl.BlockSpec((1, H, R, LANES), lambda b: (b, 0, 0, 0))
    return pl.pallas_call(
        _cumsum_kernel, grid=(B,), in_specs=[spec], out_specs=[spec] * 3,
        out_shape=[jax.ShapeDtypeStruct(x4.shape, BF16)] * 3,
        compiler_params=_params(("parallel",)), name="fox_cumsum",
    )(x4)


def _lane_group(q, lo, hi):
    lane = lax.broadcasted_iota(jnp.int32, (1, LANES), 1)
    return jnp.where((lane >= lo) & (lane < hi), q, jnp.zeros_like(q))


def _chunk_plan(qi, *, tq, tk, past, single):
    q0 = past if single else past + qi * tq
    return q0 // tk, q0


def _for_chunks(n_full, n_diag, step):
    if isinstance(n_full, int):
        for j in range(n_full):
            step(j, False)
    else:
        def body(j, c):
            step(j, False)
            return c
        lax.fori_loop(0, n_full, body, 0)
    for r in range(n_diag):
        step(n_full + r, True)


def _rows(j, tk):
    return pl.ds(j * tk, tk) if isinstance(j, int) else pl.ds(pl.multiple_of(j * tk, tk), tk)


def _softmax_init(m_ref, acc_ref):
    m_ref[...] = jnp.full(m_ref.shape, NEG_BIG, F32)
    acc_ref[...] = jnp.zeros(acc_ref.shape, F32)


def _softmax_update(s, v1, m_ref, acc_ref, c):
    m_old = m_ref[c]
    m_new = jnp.maximum(m_old, jnp.max(s, axis=1, keepdims=True))
    p = jnp.exp2(s - pltpu.repeat(m_new, s.shape[1] // LANES, axis=1))
    acc_ref[c] = jnp.exp2(m_old - m_new) * acc_ref[c] + _dot(p.astype(BF16), v1)
    m_ref[c] = m_new


def _normalised(acc):
    return acc / pltpu.roll(acc, HEAD_DIM, axis=1)


def _pair_lanes(even, odd):
    lane = lax.broadcasted_iota(jnp.int32, (1, LANES), 1)
    return jnp.where(lane < HEAD_DIM, even, pltpu.roll(odd, HEAD_DIM, axis=1))


def _fox_kernel(q_ref, k_ref, v_ref, o_ref, m_ref, acc_ref, *, tq, tk, past, single):
    n_full, q0 = _chunk_plan(pl.program_id(1), tq=tq, tk=tk, past=past, single=single)
    q_pos = q0 + lax.broadcasted_iota(jnp.int32, (tq, 1), 0)
    _softmax_init(m_ref, acc_ref)

    def step(j, masked):
        rows = _rows(j, tk)
        if masked:
            visible = j * tk + lax.broadcasted_iota(jnp.int32, (1, tk), 1) <= q_pos
        for h in range(N_HEADS):
            head = slice(LANES * h, LANES * (h + 1))
            s = _dot_nt(q_ref[0, :, head], k_ref[0, rows, head])
            if masked:
                s = jnp.where(visible, s, NEG_BIG)
            _softmax_update(s, v_ref[0, rows, head], m_ref, acc_ref, h)

    _for_chunks(n_full, tq // tk if tq > tk else 1, step)
    for pr in range(2):
        o_ref[0, :, LANES * pr:LANES * (pr + 1)] = _pair_lanes(
            _normalised(acc_ref[2 * pr]), _normalised(acc_ref[2 * pr + 1])).astype(o_ref.dtype)


def _diff_kernel(dl_ref, g_ref, q_ref, k_ref, v_ref, o_ref, m_ref, acc_ref, *, tq, tk, past, single, lam_init):
    n_full, q0 = _chunk_plan(pl.program_id(2), tq=tq, tk=tk, past=past, single=single)
    q_chunk = (q0 + lax.broadcasted_iota(jnp.int32, (tq, 1), 0)) // CHUNK
    q = q_ref[0]
    qg = [_lane_group(q, DIFF_DK * g, DIFF_DK * (g + 1)) for g in range(4)]
    _softmax_init(m_ref, acc_ref)

    def step(j, masked):
        rows = _rows(j, tk)
        k = k_ref[0, rows, :]
        if masked:
            visible = (j * tk + lax.broadcasted_iota(jnp.int32, (1, tk), 1)) // CHUNK <= q_chunk
        for g in range(4):
            s = _dot_nt(qg[g], k)
            if masked:
                s = jnp.where(visible, s, NEG_BIG)
            _softmax_update(s, v_ref[0, rows, LANES * (g // 2):LANES * (g // 2 + 1)], m_ref, acc_ref, g)

    _for_chunks(n_full, tq // tk if tq > tk else 1, step)
    dl = dl_ref[...]
    lam = (jnp.exp(jnp.sum(dl[0:1] * dl[1:2], axis=1, keepdims=True))
           - jnp.exp(jnp.sum(dl[2:3] * dl[3:4], axis=1, keepdims=True)) + lam_init)
    p = [_normalised(acc_ref[g]) for g in range(4)]
    lane = lax.broadcasted_iota(jnp.int32, (1, LANES), 1)
    first = lane < 64
    o = _pair_lanes(p[0] - lam * p[1], p[2] - lam * p[3])
    sq = o * o
    ms = jnp.where(first,
                   jnp.sum(jnp.where(first, sq, 0.0), axis=1, keepdims=True),
                   jnp.sum(jnp.where(first, 0.0, sq), axis=1, keepdims=True)) * (1.0 / HEAD_DIM)
    o_ref[0] = (o * lax.rsqrt(ms + EPS) * g_ref[...] * (1.0 - lam_init)).astype(o_ref.dtype)


def _sb_kernel(q_ref, k_ref, v_ref, o_ref, right_ref, acc_ref, *, tq, tk, sw, past, single):
    n_full, q0 = _chunk_plan(pl.program_id(1), tq=tq, tk=tk, past=past, single=single)
    q_pos = q0 + lax.broadcasted_iota(jnp.int32, (tq, 1), 0)
    ki = lax.broadcasted_iota(jnp.int32, (2 * sw, sw), 0) % sw
    kj = lax.broadcasted_iota(jnp.int32, (2 * sw, sw), 1)
    after = (ki > kj).astype(BF16)
    right_ref[...] = jnp.zeros(right_ref.shape, F32)
    acc_ref[...] = jnp.zeros(acc_ref.shape, F32)
    n_diag = tq // tk if tq > tk else 1

    def step(j, masked):
        for sub in reversed(range(tk // sw)):
            use_mask = masked and not (single and j * tk + (sub + 1) * sw <= past)
            rows = (pl.ds(j * tk + sub * sw, sw) if isinstance(j, int)
                    else pl.ds(pl.multiple_of(j * tk + sub * sw, sw), sw))
            if use_mask:
                mask = j * tk + sub * sw + lax.broadcasted_iota(jnp.int32, (1, sw), 1) < q_pos
            for h in range(N_HEADS):
                pr = slice(LANES * (h // 2), LANES * (h // 2 + 1))
                qh = _lane_group(q_ref[0, :, pr], HEAD_DIM * (h % 2), HEAD_DIM * (h % 2 + 1))
                z = _dot_nt(qh, k_ref[0, rows, pr])
                sp = jnp.maximum(z, 0.0) + jnp.log(1.0 + jnp.exp2(jnp.abs(z) * -LOG2E))
                if use_mask:
                    sp = jnp.where(mask, sp, 0.0)
                hi, lo = _split2(sp)
                later = _dot(jnp.concatenate([hi, lo], axis=1), after)
                right = right_ref[h]
                a = jnp.exp(z - sp - later - pltpu.repeat(right, sw // LANES, axis=1))
                if use_mask:
                    a = jnp.where(mask, a, 0.0)
                acc_ref[h] += _dot(a.astype(BF16), v_ref[0, rows, pr])
                right_ref[h] = right + jnp.sum(sp, axis=1, keepdims=True)

    for r in reversed(range(n_diag)):
        step(n_full + r, True)
    if isinstance(n_full, int):
        for j in reversed(range(n_full)):
            step(j, False)
    else:
        def body(i, c):
            step(n_full - 1 - i, False)
            return c
        lax.fori_loop(0, n_full, body, 0)
    lane = lax.broadcasted_iota(jnp.int32, (1, LANES), 1)
    for pr in range(2):
        o_ref[0, :, LANES * pr:LANES * (pr + 1)] = jnp.where(
            lane < 64, acc_ref[2 * pr], acc_ref[2 * pr + 1]).astype(o_ref.dtype)


def _check_tiles(Tq, Tk, tq, tk, past):
    single = Tq == tq
    assert Tq % tq == 0 and tq % CHUNK == 0 and Tk % tk == 0 and past + Tq <= Tk
    if single:
        n_full = past // tk
        assert (n_full + max(tq // tk, 1)) * tk >= past + Tq
    else:
        assert tq % tk == 0 and past % tk == 0
    return single


def _fox_attention(q, k, v, *, tq, tk, past):
    B, Tq, _ = q.shape
    Tk = k.shape[1]
    single = _check_tiles(Tq, Tk, tq, tk, past)
    keys = pl.BlockSpec((1, Tk, 4 * LANES), lambda b, i: (b, 0, 0), pipeline_mode=pl.Buffered(1))
    return pl.pallas_call(
        functools.partial(_fox_kernel, tq=tq, tk=tk, past=past, single=single),
        grid=(B, Tq // tq),
        in_specs=[pl.BlockSpec((1, tq, 4 * LANES), lambda b, i: (b, i, 0)), keys, keys],
        out_specs=pl.BlockSpec((1, tq, 2 * LANES), lambda b, i: (b, i, 0)),
        out_shape=jax.ShapeDtypeStruct((B, Tq, 2 * LANES), BF16),
        scratch_shapes=[pltpu.VMEM((N_HEADS, tq, LANES), F32), pltpu.VMEM((N_HEADS, tq, LANES), F32)],
        compiler_params=_params(("parallel", "arbitrary")),
        name="fox_attention",
    )(q, k, v)


def _diff_attention(dl, subln, q, k, v, *, tq, tk, past, lam_init):
    B, Tq, _ = q.shape
    Tk = k.shape[1]
    single = _check_tiles(Tq, Tk, tq, tk, past)
    blk = pl.BlockSpec((1, tq, LANES), lambda b, p, i: (b, i, p))
    keys = pl.BlockSpec((1, Tk, LANES), lambda b, p, i: (b, 0, p))
    values = pl.BlockSpec((1, Tk, 2 * LANES), lambda b, p, i: (b, 0, p))
    return pl.pallas_call(
        functools.partial(_diff_kernel, tq=tq, tk=tk, past=past, single=single, lam_init=lam_init),
        grid=(B, 2, Tq // tq),
        in_specs=[pl.BlockSpec((4, DIFF_DK), lambda b, p, i: (0, 0)),
                  pl.BlockSpec((1, LANES), lambda b, p, i: (0, 0)), blk, keys, values],
        out_specs=blk,
        out_shape=jax.ShapeDtypeStruct((B, Tq, 2 * LANES), BF16),
        scratch_shapes=[pltpu.VMEM((4, tq, LANES), F32), pltpu.VMEM((4, tq, LANES), F32)],
        compiler_params=_params(("parallel", "parallel", "arbitrary")),
        name="diff_attention",
    )(dl, subln, q, k, v)


def _sb_attention(q, k, v, *, tq, tk, sw, past):
    B, Tq, _ = q.shape
    Tk = k.shape[1]
    single = _check_tiles(Tq, Tk, tq, tk, past)
    assert tk % sw == 0
    kv = pl.BlockSpec((1, Tk, 2 * LANES), lambda b, i: (b, 0, 0))
    blk = pl.BlockSpec((1, tq, 2 * LANES), lambda b, i: (b, i, 0))
    return pl.pallas_call(
        functools.partial(_sb_kernel, tq=tq, tk=tk, sw=sw, past=past, single=single),
        grid=(B, Tq // tq),
        in_specs=[blk, kv, kv],
        out_specs=blk,
        out_shape=jax.ShapeDtypeStruct((B, Tq, 2 * LANES), BF16),
        scratch_shapes=[pltpu.VMEM((N_HEADS, tq, LANES), F32), pltpu.VMEM((N_HEADS, tq, LANES), F32)],
        compiler_params=_params(("parallel", "arbitrary")),
        name="sb_attention",
    )(q, k, v)


def _gdn_kernel(x_ref, z_ref, sm_ref, cw_ref, buf_ref, s0_ref, ng_ref,
                o_ref, sout_ref, cout_ref,
                xs_ref, qh_ref, kh_ref, vh_ref, u_ref, w_ref, qe_ref, ke_ref, qk_ref, gl_ref, st_ref, *, tb):
    t = pl.program_id(1)
    C = CHUNK
    tail = CONV_WIDTH - 1
    n_chunks = tb // C

    @pl.when(t == 0)
    def _():
        st_ref[...] = s0_ref[0]
        xs_ref[0:8, :] = jnp.zeros((8, GDN_CONV_CH), F32)
        xs_ref[8 - tail:8, :] = buf_ref[0]

    @pl.when(t > 0)
    def _():
        xs_ref[0:8, :] = xs_ref[tb:tb + 8, :]

    xs_ref[8:, :] = x_ref[0]
    y = cw_ref[0:1, :] * xs_ref[8 - tail:8 - tail + tb, :]
    for i in range(1, CONV_WIDTH):
        y = y + cw_ref[i:i + 1, :] * xs_ref[8 - tail + i:8 - tail + i + tb, :]
    y = y * _sigmoid(y)
    gi = lax.broadcasted_iota(jnp.int32, (256, 256), 0) // HEAD_DIM
    gj = lax.broadcasted_iota(jnp.int32, (256, 256), 1) // HEAD_DIM
    same_head = (gi == gj).astype(BF16)
    q = y[:, 0:256]
    k = y[:, 256:512]
    q = q * lax.rsqrt(_dot_sel(q * q, same_head) + EPS) * (HEAD_DIM ** -0.5)
    k = k * lax.rsqrt(_dot_sel(k * k, same_head) + EPS)
    for h in range(N_HEADS):
        cols = slice(HEAD_DIM * h, HEAD_DIM * (h + 1))
        qh_ref[h] = q[:, cols]
        kh_ref[h] = k[:, cols]
        vh_ref[h] = y[:, 512 + HEAD_DIM * h:512 + HEAD_DIM * (h + 1)]

    ci = lax.broadcasted_iota(jnp.int32, (C, C), 0)
    cj = lax.broadcasted_iota(jnp.int32, (C, C), 1)
    incl = ci >= cj
    strict = ci > cj
    incl_b = incl.astype(BF16)
    upto_f = (ci <= cj).astype(F32)
    ones_b = jnp.ones((C, C), BF16)
    eye = (ci == cj).astype(F32)

    def prepare(c):
        rows = pl.ds(c * C, C) if isinstance(c, int) else pl.ds(pl.multiple_of(c * C, C), C)
        sm = sm_ref[0, rows, :]
        gcum_all = _dot_sel(sm, incl_b, x_first=False)
        gl_rows = pl.ds(c * 8, 8) if isinstance(c, int) else pl.ds(pl.multiple_of(c * 8, 8), 8)
        gl_ref[gl_rows, :] = jnp.broadcast_to(gcum_all[C - 1:C, :], (8, LANES))
        for h in range(N_HEADS):
            qh = qh_ref[h, rows, :]
            kh = kh_ref[h, rows, :]
            g_col = sm[:, 4 + h:5 + h]
            beta = sm[:, 8 + h:9 + h]
            gcum = gcum_all[:, 4 + h:5 + h]
            gcum_row = _dot_sel(jnp.broadcast_to(g_col, (C, C)) * upto_f, ones_b, x_first=False)
            decay = jnp.where(incl, jnp.exp(jnp.where(incl, gcum - gcum_row, 0.0)), 0.0)
            kb = kh * beta
            m = jnp.where(strict, _dot3(kb, kh, _NT) * decay, 0.0)
            pw = -m
            inv = eye + pw
            for _ in range(5):
                pw = _dot3(pw, pw)
                inv = inv + _dot3(inv, pw)
            eg = jnp.exp(gcum)
            g_last = gcum[C - 1:C, :]
            u_ref[h, rows, :] = _dot3(inv, vh_ref[h, rows, :] * beta)
            w_ref[h, rows, :] = _dot3(inv, kb * eg)
            qk_ref[h, rows, :] = jnp.where(incl, _dot3(qh, kh, _NT) * decay, 0.0)
            qe_ref[h, rows, :] = qh * eg
            ke_ref[h, rows, :] = kh * jnp.exp(g_last - gcum)

    if n_chunks % 2:
        prepare(n_chunks - 1)
    if n_chunks >= 2:
        def prep_pair(i, c):
            prepare(2 * i)
            prepare(2 * i + 1)
            return c
        lax.fori_loop(0, n_chunks // 2, prep_pair, 0)

    def scan(c, states):
        rows = pl.ds(c * C, C) if isinstance(c, int) else pl.ds(pl.multiple_of(c * C, C), C)
        gl_rows = pl.ds(c * 8, 8) if isinstance(c, int) else pl.ds(pl.multiple_of(c * 8, 8), 8)
        gl = gl_ref[gl_rows, :]
        new_states = []
        outs = []
        for h in range(N_HEADS):
            S = states[h]
            v_new = u_ref[h, rows, :] - _dot3(w_ref[h, rows, :], S)
            o = _dot3(qe_ref[h, rows, :], S) + _dot3(qk_ref[h, rows, :], v_new)
            S = S * jnp.exp(gl[0:1, 4 + h:5 + h]) + _dot3(ke_ref[h, rows, :], v_new, _TN)
            new_states.append(S)
            zh = z_ref[0, rows, HEAD_DIM * h:HEAD_DIM * (h + 1)]
            outs.append(_rms(o, ng_ref[...]) * (zh * _sigmoid(zh)))
        o_ref[0, rows, :] = jnp.concatenate(outs, axis=1).astype(o_ref.dtype)
        return tuple(new_states)

    states = tuple(st_ref[h] for h in range(N_HEADS))
    if n_chunks == 1:
        states = scan(0, states)
    else:
        states = lax.fori_loop(0, n_chunks, scan, states)
    for h in range(N_HEADS):
        st_ref[h] = states[h]

    @pl.when(t == pl.num_programs(1) - 1)
    def _():
        for h in range(N_HEADS):
            sout_ref[0, h] = states[h]
        cout_ref[0] = xs_ref[8 + tb - tail:8 + tb, :]


def _gdn(x, z, small, conv_w, buf, s0, norm_g, tb):
    B, T, _ = x.shape
    assert T % tb == 0 and tb % CHUNK == 0 and tb >= 8
    tok = lambda w: pl.BlockSpec((1, tb, w), lambda b, t: (b, t, 0))
    per_b = lambda s: pl.BlockSpec((1,) + s, lambda b, t: (b,) + (0,) * len(s))
    per_head = pltpu.VMEM((N_HEADS, tb, HEAD_DIM), F32)
    return pl.pallas_call(
        functools.partial(_gdn_kernel, tb=tb),
        grid=(B, T // tb),
        in_specs=[tok(768), tok(256), tok(128),
                  pl.BlockSpec((CONV_WIDTH, 768), lambda b, t: (0, 0)),
                  per_b((CONV_WIDTH - 1, 768)), per_b((N_HEADS, 64, 64)),
                  pl.BlockSpec((1, 64), lambda b, t: (0, 0))],
        out_specs=[tok(256), per_b((N_HEADS, 64, 64)), per_b((CONV_WIDTH - 1, 768))],
        out_shape=[jax.ShapeDtypeStruct((B, T, 256), BF16),
                   jax.ShapeDtypeStruct((B, N_HEADS, 64, 64), F32),
                   jax.ShapeDtypeStruct((B, CONV_WIDTH - 1, 768), F32)],
        scratch_shapes=[pltpu.VMEM((tb + 8, 768), F32)] + [per_head] * 8
                       + [pltpu.VMEM((tb // CHUNK * 8, LANES), F32), pltpu.VMEM((N_HEADS, 64, 64), F32)],
        compiler_params=_params(("parallel", "arbitrary")),
        name="gated_deltanet",
    )(x, z, small, conv_w, buf, s0, norm_g)


def _merge_kernel(x_ref, g1_ref, g2_ref, of_ref, od_ref, os_ref, og_ref, wgate_ref, wb_ref, wo_ref,
                  x1_ref, h_ref):
    x = x_ref[...]
    u = _rms(x, g1_ref[...]).astype(BF16)
    merged = None
    for i, o_ref in enumerate((of_ref, od_ref, os_ref, og_ref)):
        gate = _sigmoid(_dot(u, wgate_ref[:, i * D_MODEL:(i + 1) * D_MODEL]))
        term = gate * _dot(o_ref[...], wb_ref[i])
        merged = term if merged is None else merged + term
    x1 = x + _dot(merged.astype(BF16), wo_ref[...])
    x1_ref[...] = x1
    h_ref[...] = _rms(x1, g2_ref[...]).astype(BF16)


def _merge(x2, g1, g2, o_fox, o_diff, o_sb, o_gdn, w_gate, w_branch, w_out, tm):
    M = x2.shape[0]
    row = lambda w: pl.BlockSpec((tm, w), lambda i: (i, 0))
    return pl.pallas_call(
        _merge_kernel,
        grid=(M // tm,),
        in_specs=[row(D_MODEL), _const_spec((1, D_MODEL)), _const_spec((1, D_MODEL)),
                  row(256), row(256), row(256), row(256),
                  _const_spec((D_MODEL, N_BRANCH * D_MODEL)), _const_spec((N_BRANCH, BRANCH_W, D_MODEL)),
                  _const_spec((D_MODEL, D_MODEL))],
        out_specs=[row(D_MODEL), row(D_MODEL)],
        out_shape=[jax.ShapeDtypeStruct((M, D_MODEL), F32), jax.ShapeDtypeStruct((M, D_MODEL), BF16)],
        compiler_params=_params(("parallel",)),
        name="branch_merge",
    )(x2, g1, g2, o_fox, o_diff, o_sb, o_gdn, w_gate, w_branch, w_out)


def _ffn_kernel(x1_ref, h_ref, wg_ref, wu_ref, wd_ref, gf_ref, o_ref, *, final):
    h = h_ref[...]
    acc = x1_ref[...]
    half = D_FF // 2
    for f0 in (0, half):
        a = _dot(h, wg_ref[:, f0:f0 + half])
        b = _dot(h, wu_ref[:, f0:f0 + half])
        acc = acc + _dot((a * _sigmoid(a) * b).astype(BF16), wd_ref[f0:f0 + half, :])
    if final:
        acc = _rms(acc, gf_ref[...])
    o_ref[...] = acc


def _ffn(x1, h, wg, wu, wd, g_final, final, tm):
    M = x1.shape[0]
    row = pl.BlockSpec((tm, D_MODEL), lambda i: (i, 0))
    return pl.pallas_call(
        functools.partial(_ffn_kernel, final=final),
        grid=(M // tm,),
        in_specs=[row, row, _const_spec((D_MODEL, D_FF)), _const_spec((D_MODEL, D_FF)),
                  _const_spec((D_FF, D_MODEL)), _const_spec((1, D_MODEL))],
        out_specs=row,
        out_shape=jax.ShapeDtypeStruct((M, D_MODEL), F32),
        compiler_params=_params(("parallel",)),
        name="swiglu",
    )(x1, h, wg, wu, wd, g_final)


def _prep_layer(p, l):
    w = p["w_in"][l]
    cols = lambda a, b: w[:, a:b]
    small = jnp.concatenate([cols(_FOX0 + 768, _FOX0 + 772), cols(_GDN0 + 768, _GDN0 + 776),
                             jnp.zeros((D_MODEL, LANES - 12), F32)], axis=1)
    w_r = jnp.concatenate([cols(_FOX0, _FOX0 + 768), cols(_DIF0, _DIF0 + 768), cols(_SB0, _SB0 + 768),
                           cols(_GDN0, _GDN0 + 768), cols(_GDN0 + 776, _GDN0 + 1032), small], axis=1).astype(BF16)
    sp = jnp.zeros((8, LANES), F32)
    sp = sp.at[0, 0:4].set(p["b_fox_f"][l]).at[0, 4:8].set(p["gdn_dt_bias"][l]).at[1, 4:8].set(p["gdn_a_log"][l])
    return dict(
        w_r=w_r, sp=sp, w_gate=cols(_GATE0, _GATE0 + N_BRANCH * D_MODEL).astype(BF16),
        g1=p["norm1_g"][l].reshape(1, D_MODEL), g2=p["norm2_g"][l].reshape(1, D_MODEL),
        dl=jnp.stack([p["diff_lq1"][l], p["diff_lk1"][l], p["diff_lq2"][l], p["diff_lk2"][l]]),
        subln=jnp.tile(p["diff_subln_g"][l], 2).reshape(1, LANES),
        conv_w=p["gdn_conv_w"][l], norm_g=p["gdn_norm_g"][l].reshape(1, 64),
        w_branch=p["w_branch"][l].astype(BF16), w_out=p["w_out"][l].astype(BF16),
        wg=p["w_ffn_gate"][l].astype(BF16), wu=p["w_ffn_up"][l].astype(BF16), wd=p["w_ffn_down"][l].astype(BF16),
    )


def _with_past(past, new16, t_pad):
    B, T, W = new16.shape
    parts = [new16] if past is None else [past.reshape(B, -1, W).astype(BF16), new16]
    n = sum(a.shape[1] for a in parts)
    if t_pad > n:
        parts.append(jnp.zeros((B, t_pad - n, W), BF16))
    return jnp.concatenate(parts, axis=1) if len(parts) > 1 else parts[0]


def _with_forget_lanes(x16, c3, query):
    B, T, _ = x16.shape
    pieces = jnp.stack([jnp.swapaxes(a if query else -a, 1, 2) for a in c3], axis=-1)
    ones = jnp.ones((B, T, N_HEADS, 3), BF16)
    zeros = jnp.zeros((B, T, N_HEADS, HEAD_DIM - 6), BF16)
    extra = [pieces, ones, zeros] if query else [ones, pieces, zeros]
    return jnp.concatenate([x16.reshape(B, T, N_HEADS, HEAD_DIM)] + extra, axis=-1).reshape(B, T, N_HEADS * LANES)


def _with_ones_lanes(v16):
    B, T, _ = v16.shape
    ones = jnp.ones((B, T, N_HEADS, HEAD_DIM), BF16)
    return jnp.concatenate([v16.reshape(B, T, N_HEADS, HEAD_DIM), ones], axis=-1).reshape(B, T, N_HEADS * LANES)


def _trunk(x, caches, p, layers, *, tm, tq, tk, sw, tb):
    B, T, _ = x.shape
    M = B * T
    past = 0 if caches is None else caches[0].shape[2]
    t_keys = -(-(past + T) // tk) * tk
    x2 = x.reshape(M, D_MODEL)
    new_state = []
    for l in range(DEPTH):
        lp = layers[l]
        c = None if caches is None else tuple(a[l] for a in caches)
        (fq, fk32, fk16, fv32, fv16, dq, dk32, dk16, dv32, dv16, sq, sk32, sk16, sv32, sv16,
         gqkv, gz, small) = _in_proj(x2, lp["g1"], lp["w_r"], lp["sp"], tm)
        b3 = lambda a: a.reshape(B, T, a.shape[-1])
        kv = lambda i, a: _with_past(None if c is None else c[i], b3(a), t_keys)

        logf = b3(small)[:, :, 0:N_HEADS]
        lf_all = logf if c is None else jnp.concatenate([c[2], logf], axis=1)
        r_pad = -(-t_keys // (16 * LANES)) * 16
        lf_all = jnp.pad(lf_all, ((0, 0), (0, r_pad * LANES - lf_all.shape[1]), (0, 0)))
        cum3 = [a.reshape(B, N_HEADS, -1)
                for a in _cumsum_time(jnp.swapaxes(lf_all, 1, 2).reshape(B, N_HEADS, r_pad, LANES))]
        fq_c = _with_forget_lanes(b3(fq), [a[:, :, past:past + T] for a in cum3], True)
        fk_c = _with_forget_lanes(kv(0, fk16), [a[:, :, :t_keys] for a in cum3], False)
        o_fox = _fox_attention(fq_c, fk_c, _with_ones_lanes(kv(1, fv16)), tq=tq, tk=tk, past=past)

        lam_init = 0.8 - 0.6 * math.exp(-0.3 * l)
        o_diff = _diff_attention(lp["dl"], lp["subln"], b3(dq), kv(3, dk16), _with_ones_lanes(kv(4, dv16)),
                                 tq=tq, tk=tk, past=past, lam_init=lam_init)

        o_sb = _sb_attention(b3(sq), kv(5, sk16), kv(6, sv16), tq=tq, tk=tk, sw=sw, past=past)

        buf = jnp.zeros((B, CONV_WIDTH - 1, GDN_CONV_CH), F32) if c is None else c[8]
        s0 = jnp.zeros((B, N_HEADS, 64, 64), F32) if c is None else c[7]
        o_gdn, s_new, conv_state = _gdn(b3(gqkv), b3(gz), b3(small), lp["conv_w"], buf, s0, lp["norm_g"], tb)

        x1, h = _merge(x2, lp["g1"], lp["g2"], o_fox.reshape(M, 256), o_diff.reshape(M, 256),
                       o_sb.reshape(M, 256), o_gdn.reshape(M, 256), lp["w_gate"], lp["w_branch"], lp["w_out"], tm)
        x2 = _ffn(x1, h, lp["wg"], lp["wu"], lp["wd"], p["final_norm_g"].reshape(1, D_MODEL),
                  l == DEPTH - 1, tm)
        h4 = lambda a: a.reshape(B, T, N_HEADS, HEAD_DIM)
        new_state.append((h4(fk32), h4(fv32), logf, h4(dk32), h4(dv32), h4(sk32), h4(sv32), s_new, conv_state))
    y = x2.reshape(B, T, D_MODEL)
    return y, tuple(jnp.stack(parts) for parts in zip(*new_state))


def kernel(x_prompt, x_sample, cache_fox_k, cache_fox_v, cache_fox_logf, cache_diff_k, cache_diff_v, cache_sb_k, cache_sb_v, state_gdn, state_gdn_conv, norm1_g, w_in, b_fox_f, diff_lq1, diff_lk1, diff_lq2, diff_lk2, diff_subln_g, gdn_conv_w, gdn_a_log, gdn_dt_bias, gdn_norm_g, w_branch, w_out, norm2_g, w_ffn_gate, w_ffn_up, w_ffn_down, final_norm_g):
    p = dict(norm1_g=norm1_g, w_in=w_in, b_fox_f=b_fox_f, diff_lq1=diff_lq1, diff_lk1=diff_lk1, diff_lq2=diff_lq2,
             diff_lk2=diff_lk2, diff_subln_g=diff_subln_g, gdn_conv_w=gdn_conv_w, gdn_a_log=gdn_a_log,
             gdn_dt_bias=gdn_dt_bias, gdn_norm_g=gdn_norm_g, w_branch=w_branch, w_out=w_out, norm2_g=norm2_g,
             w_ffn_gate=w_ffn_gate, w_ffn_up=w_ffn_up, w_ffn_down=w_ffn_down, final_norm_g=final_norm_g)
    layers = [_prep_layer(p, l) for l in range(DEPTH)]
    t_p = x_prompt.shape[1]
    blk = min(512, t_p)
    y_prompt, sp = _trunk(x_prompt, None, p, layers, tm=blk, tq=blk, tk=blk, sw=min(256, blk), tb=blk)
    caches = (cache_fox_k, cache_fox_v, cache_fox_logf, cache_diff_k, cache_diff_v, cache_sb_k, cache_sb_v,
              state_gdn, state_gdn_conv)
    b_s, t_s = x_sample.shape[0], x_sample.shape[1]
    keys_s = -(-(cache_fox_k.shape[2] + t_s) // LANES) * LANES
    y_sample, ss = _trunk(x_sample, caches, p, layers, tm=min(512, b_s * t_s), tq=t_s, tk=keys_s, sw=LANES, tb=t_s)
    return (y_prompt, y_sample) + sp + ss
```

```python
import functools
import math

import jax
import jax.numpy as jnp
from jax import lax
from jax.experimental import pallas as pl
from jax.experimental.pallas import tpu as pltpu

F32 = jnp.float32
BF16 = jnp.bfloat16

D_MODEL = 1024
DEPTH = 2
CHUNK = 64
HEAD_DIM = 64
N_HEADS = 4
DIFF_DK = 32
CONV_WIDTH = 4
BRANCH_W = 256
N_BRANCH = 4
D_FF = 2816
EPS = 1e-6
GDN_CONV_CH = 768

LANES = 128
NEG_BIG = -1e30
VMEM_LIMIT = 56 * 1024 * 1024
LOG2E = 1.4426950408889634
SB_CUTOFF = 105.0

_FOX0, _DIF0, _SB0, _GDN0, _GATE0 = 0, 772, 1540, 2308, 3340
N_PROJ = 3456

_NN = (((1,), (0,)), ((), ()))
_NT = (((1,), (1,)), ((), ()))
_TN = (((0,), (0,)), ((), ()))


def _dot(a, b):
    return jnp.dot(a, b, preferred_element_type=F32)


def _dot_nt(a, b):
    return lax.dot_general(a, b, _NT, preferred_element_type=F32)


def _split2(a):
    hi = a.astype(BF16)
    return hi, (a - hi.astype(F32)).astype(BF16)


def _dot_sel(x, sel, x_first=True):
    hi = x.astype(BF16)
    r = x - hi.astype(F32)
    mid = r.astype(BF16)
    lo = (r - mid.astype(F32)).astype(BF16)
    if x_first:
        return _dot(hi, sel) + (_dot(mid, sel) + _dot(lo, sel))
    return _dot(sel, hi) + (_dot(sel, mid) + _dot(sel, lo))


def _rms(x, g):
    return x * lax.rsqrt(jnp.mean(x * x, axis=-1, keepdims=True) + EPS) * g


def _softplus(t):
    return jnp.maximum(t, 0.0) + jnp.log1p(jnp.exp(-jnp.abs(t)))


def _sigmoid(t):
    return 1.0 / (1.0 + jnp.exp(-t))


def _const_spec(shape):
    nd = len(shape)
    return pl.BlockSpec(shape, lambda *_: (0,) * nd, pipeline_mode=pl.Buffered(1))


def _params(sem):
    return pltpu.CompilerParams(dimension_semantics=sem, vmem_limit_bytes=VMEM_LIMIT)


def _in_proj_kernel(x_ref, g_ref, w_ref, sp_ref,
                    fq_ref, fk32_ref, fk16_ref, fv32_ref, fv16_ref,
                    dq_ref, dk32_ref, dk16_ref, dv32_ref, dv16_ref,
                    sq_ref, sk32_ref, sk16_ref, sv32_ref, sv16_ref,
                    gqkv_ref, gz_ref, small_ref):
    u = _rms(x_ref[...], g_ref[...]).astype(BF16)

    def mm(c0, w):
        return _dot(u, w_ref[:, c0:c0 + w])

    def qkv(c0, q_scale, q_ref, k32, k16, v32, v16):
        q_ref[...] = (mm(c0, 256) * q_scale).astype(BF16)
        k = mm(c0 + 256, 256)
        k32[...] = k
        k16[...] = k.astype(BF16)
        v = mm(c0 + 512, 256)
        v32[...] = v
        v16[...] = v.astype(BF16)

    qkv(0, HEAD_DIM ** -0.5 * LOG2E, fq_ref, fk32_ref, fk16_ref, fv32_ref, fv16_ref)
    qkv(768, DIFF_DK ** -0.5 * LOG2E, dq_ref, dk32_ref, dk16_ref, dv32_ref, dv16_ref)
    qkv(1536, HEAD_DIM ** -0.5, sq_ref, sk32_ref, sk16_ref, sv32_ref, sv16_ref)
    gqkv_ref[...] = mm(2304, 768)
    gz_ref[...] = mm(3072, 256)
    t = mm(3328, 128) + sp_ref[0:1, :]
    lane = lax.broadcasted_iota(jnp.int32, (1, LANES), 1)
    sp = _softplus(t)
    logf = t - sp
    g = -jnp.exp(sp_ref[1:2, :]) * sp
    beta = _sigmoid(t)
    small_ref[...] = jnp.where(lane < 4, logf, jnp.where(lane < 8, g, jnp.where(lane < 12, beta, 0.0)))


def _in_proj(x2, g1, w_r, sp, tm):
    M = x2.shape[0]
    row = lambda w: pl.BlockSpec((tm, w), lambda i: (i, 0))
    f32o = lambda w: jax.ShapeDtypeStruct((M, w), F32)
    b16o = lambda w: jax.ShapeDtypeStruct((M, w), BF16)
    qkv_shapes = [b16o(256), f32o(256), b16o(256), f32o(256), b16o(256)]
    out_shape = qkv_shapes * 3 + [f32o(768), f32o(256), f32o(128)]
    out_specs = [row(256)] * 15 + [row(768), row(256), row(128)]
    return pl.pallas_call(
        _in_proj_kernel,
        grid=(M // tm,),
        in_specs=[row(D_MODEL), _const_spec((1, D_MODEL)), _const_spec((D_MODEL, N_PROJ)), _const_spec((8, LANES))],
        out_specs=out_specs,
        out_shape=out_shape,
        compiler_params=_params(("parallel",)),
        name="in_proj",
    )(x2, g1, w_r, sp)


def _cumsum_kernel(x_ref, hi_ref, mid_ref, lo_ref):
    R = x_ref.shape[2]
    ii = lax.broadcasted_iota(jnp.int32, (LANES, LANES), 0)
    jj = lax.broadcasted_iota(jnp.int32, (LANES, LANES), 1)
    upper = (ii <= jj).astype(BF16)
    ri = lax.broadcasted_iota(jnp.int32, (R, R), 0)
    rj = lax.broadcasted_iota(jnp.int32, (R, R), 1)
    lower = (rj < ri).astype(BF16)
    for h in range(N_HEADS):
        within = _dot_sel(x_ref[0, h], upper)
        tot = jnp.broadcast_to(within[:, LANES - 1:LANES], (R, LANES))
        c = (within + _dot_sel(tot, lower, x_first=False)) * LOG2E
        hi = c.astype(BF16)
        r = c - hi.astype(F32)
        mid = r.astype(BF16)
        hi_ref[0, h] = hi
        mid_ref[0, h] = mid
        lo_ref[0, h] = (r - mid.astype(F32)).astype(BF16)


def _cumsum_time(x4):
    B, H, R, _ = x4.shape
    spec = pl.BlockSpec((1, H, R, LANES), lambda b: (b, 0, 0, 0))
    return pl.pallas_call(
        _cumsum_kernel, grid=(B,), in_specs=[spec], out_specs=[spec] * 3,
        out_shape=[jax.ShapeDtypeStruct(x4.shape, BF16)] * 3,
        compiler_params=_params(("parallel",)), name="fox_cumsum",
    )(x4)


def _lane_group(q, lo, hi):
    lane = lax.broadcasted_iota(jnp.int32, (1, LANES), 1)
    return jnp.where((lane >= lo) & (lane < hi), q, jnp.zeros_like(q))


def _chunk_plan(qi, *, tq, tk, past, single):
    q0 = past if single else past + qi * tq
    return q0 // tk, q0


def _for_chunks(n_full, n_diag, step):
    if isinstance(n_full, int):
        for j in range(n_full):
            step(j, False)
    else:
        def body(j, c):
            step(j, False)
            return c
        lax.fori_loop(0, n_full, body, 0)
    for r in range(n_diag):
        step(n_full + r, True)


def _rows(j, tk):
    return pl.ds(j * tk, tk) if isinstance(j, int) else pl.ds(pl.multiple_of(j * tk, tk), tk)


def _lane_tile(x, reps):
    return x if reps == 1 else jnp.concatenate([x] * reps, axis=1)


def _softmax_init(m_ref, acc_ref):
    m_ref[...] = jnp.full(m_ref.shape, NEG_BIG, F32)
    acc_ref[...] = jnp.zeros(acc_ref.shape, F32)


def _softmax_update(s, v1, m_ref, acc_ref, c):
    m_old = m_ref[c]
    m_new = jnp.maximum(m_old, jnp.max(s, axis=1, keepdims=True))
    p = jnp.exp2(s - _lane_tile(m_new, s.shape[1] // LANES))
    acc_ref[c] = jnp.exp2(m_old - m_new) * acc_ref[c] + _dot(p.astype(BF16), v1)
    m_ref[c] = m_new


def _normalised(acc):
    return acc / pltpu.roll(acc, HEAD_DIM, axis=1)


def _pair_lanes(even, odd):
    lane = lax.broadcasted_iota(jnp.int32, (1, LANES), 1)
    return jnp.where(lane < HEAD_DIM, even, pltpu.roll(odd, HEAD_DIM, axis=1))


def _fox_kernel(q_ref, k_ref, v_ref, o_ref, m_ref, acc_ref, *, tq, tk, past, single):
    n_full, q0 = _chunk_plan(pl.program_id(1), tq=tq, tk=tk, past=past, single=single)
    q_pos = q0 + lax.broadcasted_iota(jnp.int32, (tq, 1), 0)
    _softmax_init(m_ref, acc_ref)

    def step(j, masked):
        rows = _rows(j, tk)
        if masked:
            visible = j * tk + lax.broadcasted_iota(jnp.int32, (1, tk), 1) <= q_pos
        for h in range(N_HEADS):
            head = slice(LANES * h, LANES * (h + 1))
            s = _dot_nt(q_ref[0, :, head], k_ref[0, rows, head])
            if masked:
                s = jnp.where(visible, s, NEG_BIG)
            _softmax_update(s, v_ref[0, rows, head], m_ref, acc_ref, h)

    _for_chunks(n_full, tq // tk if tq > tk else 1, step)
    for pr in range(2):
        o_ref[0, :, LANES * pr:LANES * (pr + 1)] = _pair_lanes(
            _normalised(acc_ref[2 * pr]), _normalised(acc_ref[2 * pr + 1])).astype(o_ref.dtype)


def _diff_kernel(dl_ref, g_ref, q_ref, k_ref, v_ref, o_ref, m_ref, acc_ref, *, tq, tk, past, single, lam_init):
    n_full, q0 = _chunk_plan(pl.program_id(2), tq=tq, tk=tk, past=past, single=single)
    q_chunk = (q0 + lax.broadcasted_iota(jnp.int32, (tq, 1), 0)) // CHUNK
    q = q_ref[0]
    qg = [_lane_group(q, DIFF_DK * g, DIFF_DK * (g + 1)) for g in range(4)]
    _softmax_init(m_ref, acc_ref)

    def step(j, masked):
        rows = _rows(j, tk)
        k = k_ref[0, rows, :]
        if masked:
            visible = (j * tk + lax.broadcasted_iota(jnp.int32, (1, tk), 1)) // CHUNK <= q_chunk
        for g in range(4):
            s = _dot_nt(qg[g], k)
            if masked:
                s = jnp.where(visible, s, NEG_BIG)
            _softmax_update(s, v_ref[0, rows, LANES * (g // 2):LANES * (g // 2 + 1)], m_ref, acc_ref, g)

    _for_chunks(n_full, tq // tk if tq > tk else 1, step)
    dl = dl_ref[...]
    lam = (jnp.exp(jnp.sum(dl[0:1] * dl[1:2], axis=1, keepdims=True))
           - jnp.exp(jnp.sum(dl[2:3] * dl[3:4], axis=1, keepdims=True)) + lam_init)
    p = [_normalised(acc_ref[g]) for g in range(4)]
    lane = lax.broadcasted_iota(jnp.int32, (1, LANES), 1)
    first = lane < 64
    o = _pair_lanes(p[0] - lam * p[1], p[2] - lam * p[3])
    sq = o * o
    ms = jnp.where(first,
                   jnp.sum(jnp.where(first, sq, 0.0), axis=1, keepdims=True),
                   jnp.sum(jnp.where(first, 0.0, sq), axis=1, keepdims=True)) * (1.0 / HEAD_DIM)
    o_ref[0] = (o * lax.rsqrt(ms + EPS) * g_ref[...] * (1.0 - lam_init)).astype(o_ref.dtype)


def _sb_kernel(q_ref, k_ref, v_ref, o_ref, right_ref, acc_ref, *, tq, tk, sw, past, single):
    n_full, q0 = _chunk_plan(pl.program_id(1), tq=tq, tk=tk, past=past, single=single)
    q_pos = q0 + lax.broadcasted_iota(jnp.int32, (tq, 1), 0)
    ki = lax.broadcasted_iota(jnp.int32, (2 * sw, sw), 0) % sw
    kj = lax.broadcasted_iota(jnp.int32, (2 * sw, sw), 1)
    after = (ki > kj).astype(BF16)
    right_ref[...] = jnp.zeros(right_ref.shape, F32)
    acc_ref[...] = jnp.zeros(acc_ref.shape, F32)
    n_diag = tq // tk if tq > tk else 1

    def step(j, masked):
        for sub in reversed(range(tk // sw)):
            use_mask = masked and not (single and j * tk + (sub + 1) * sw <= past)
            rows = (pl.ds(j * tk + sub * sw, sw) if isinstance(j, int)
                    else pl.ds(pl.multiple_of(j * tk + sub * sw, sw), sw))
            if use_mask:
                mask = j * tk + sub * sw + lax.broadcasted_iota(jnp.int32, (1, sw), 1) < q_pos
            for h in range(N_HEADS):
                pr = slice(LANES * (h // 2), LANES * (h // 2 + 1))
                qh = _lane_group(q_ref[0, :, pr], HEAD_DIM * (h % 2), HEAD_DIM * (h % 2 + 1))
                z = _dot_nt(qh, k_ref[0, rows, pr])
                sp = jnp.maximum(z, 0.0) + jnp.log(1.0 + jnp.exp2(jnp.abs(z) * -LOG2E))
                if use_mask:
                    sp = jnp.where(mask, sp, 0.0)
                hi, lo = _split2(sp)
                later = _dot(jnp.concatenate([hi, lo], axis=1), after)
                right = right_ref[h]
                a = jnp.exp(z - sp - later - _lane_tile(right, sw // LANES))
                if use_mask:
                    a = jnp.where(mask, a, 0.0)
                acc_ref[h] += _dot(a.astype(BF16), v_ref[0, rows, pr])
                right_ref[h] = right + jnp.sum(sp, axis=1, keepdims=True)

    for r in reversed(range(n_diag)):
        step(n_full + r, True)
    if isinstance(n_full, int):
        for j in reversed(range(n_full)):
            step(j, False)
    else:
        def more(c):
            return (c[0] < n_full) & c[1]

        def body(c):
            step(n_full - 1 - c[0], False)
            return c[0] + 1, jnp.min(right_ref[...]) < SB_CUTOFF
        lax.while_loop(more, body, (jnp.int32(0), jnp.min(right_ref[...]) < SB_CUTOFF))
    lane = lax.broadcasted_iota(jnp.int32, (1, LANES), 1)
    for pr in range(2):
        o_ref[0, :, LANES * pr:LANES * (pr + 1)] = jnp.where(
            lane < 64, acc_ref[2 * pr], acc_ref[2 * pr + 1]).astype(o_ref.dtype)


def _check_tiles(Tq, Tk, tq, tk, past):
    single = Tq == tq
    assert Tq % tq == 0 and tq % CHUNK == 0 and Tk % tk == 0 and past + Tq <= Tk
    if single:
        n_full = past // tk
        assert (n_full + max(tq // tk, 1)) * tk >= past + Tq
    else:
        assert tq % tk == 0 and past % tk == 0
    return single


def _fox_attention(q, k, v, *, tq, tk, past):
    B, Tq, _ = q.shape
    Tk = k.shape[1]
    single = _check_tiles(Tq, Tk, tq, tk, past)
    keys = pl.BlockSpec((1, Tk, 4 * LANES), lambda b, i: (b, 0, 0), pipeline_mode=pl.Buffered(1))
    return pl.pallas_call(
        functools.partial(_fox_kernel, tq=tq, tk=tk, past=past, single=single),
        grid=(B, Tq // tq),
        in_specs=[pl.BlockSpec((1, tq, 4 * LANES), lambda b, i: (b, i, 0)), keys, keys],
        out_specs=pl.BlockSpec((1, tq, 2 * LANES), lambda b, i: (b, i, 0)),
        out_shape=jax.ShapeDtypeStruct((B, Tq, 2 * LANES), BF16),
        scratch_shapes=[pltpu.VMEM((N_HEADS, tq, LANES), F32), pltpu.VMEM((N_HEADS, tq, LANES), F32)],
        compiler_params=_params(("parallel", "arbitrary")),
        name="fox_attention",
    )(q, k, v)


def _diff_attention(dl, subln, q, k, v, *, tq, tk, past, lam_init):
    B, Tq, _ = q.shape
    Tk = k.shape[1]
    single = _check_tiles(Tq, Tk, tq, tk, past)
    blk = pl.BlockSpec((1, tq, LANES), lambda b, p, i: (b, i, p))
    keys = pl.BlockSpec((1, Tk, LANES), lambda b, p, i: (b, 0, p))
    values = pl.BlockSpec((1, Tk, 2 * LANES), lambda b, p, i: (b, 0, p))
    return pl.pallas_call(
        functools.partial(_diff_kernel, tq=tq, tk=tk, past=past, single=single, lam_init=lam_init),
        grid=(B, 2, Tq // tq),
        in_specs=[pl.BlockSpec((4, DIFF_DK), lambda b, p, i: (0, 0)),
                  pl.BlockSpec((1, LANES), lambda b, p, i: (0, 0)), blk, keys, values],
        out_specs=blk,
        out_shape=jax.ShapeDtypeStruct((B, Tq, 2 * LANES), BF16),
        scratch_shapes=[pltpu.VMEM((4, tq, LANES), F32), pltpu.VMEM((4, tq, LANES), F32)],
        compiler_params=_params(("parallel", "parallel", "arbitrary")),
        name="diff_attention",
    )(dl, subln, q, k, v)


def _sb_attention(q, k, v, *, tq, tk, sw, past):
    B, Tq, _ = q.shape
    Tk = k.shape[1]
    single = _check_tiles(Tq, Tk, tq, tk, past)
    assert tk % sw == 0
    kv = pl.BlockSpec((1, Tk, 2 * LANES), lambda b, i: (b, 0, 0))
    blk = pl.BlockSpec((1, tq, 2 * LANES), lambda b, i: (b, i, 0))
    return pl.pallas_call(
        functools.partial(_sb_kernel, tq=tq, tk=tk, sw=sw, past=past, single=single),
        grid=(B, Tq // tq),
        in_specs=[blk, kv, kv],
        out_specs=blk,
        out_shape=jax.ShapeDtypeStruct((B, Tq, 2 * LANES), BF16),
        scratch_shapes=[pltpu.VMEM((N_HEADS, tq, LANES), F32), pltpu.VMEM((N_HEADS, tq, LANES), F32)],
        compiler_params=_params(("parallel", "arbitrary")),
        name="sb_attention",
    )(q, k, v)


def _mm3(a2, b2, dims=_NN):
    (ah, al), (bh, bl) = a2, b2
    dg = lambda x, y: lax.dot_general(x, y, dims, preferred_element_type=F32)
    return dg(ah, bh) + (dg(ah, bl) + dg(al, bh))


def _head_blocks(x):
    lane_head = lax.broadcasted_iota(jnp.int32, (1, 256), 1) // HEAD_DIM
    return jnp.concatenate([jnp.where(lane_head == h, x, 0.0) for h in range(N_HEADS)], axis=0)


def _fold_blocks(x):
    C = CHUNK
    return (x[0:C] + x[C:2 * C]) + (x[2 * C:3 * C] + x[3 * C:4 * C])


def _gdn_kernel(x_ref, z_ref, sm_ref, cw_ref, buf_ref, s0_ref, ng_ref,
                o_ref, sout_ref, cout_ref,
                xs_ref, qkv_ref, u_ref, w_ref, qe_ref, ke_ref, qk_ref, gl_ref, st_ref, *, tb):
    t = pl.program_id(1)
    C = CHUNK
    W = N_HEADS * HEAD_DIM
    tail = CONV_WIDTH - 1
    n_chunks = tb // C

    ri = lax.broadcasted_iota(jnp.int32, (W, W), 0)
    rj = lax.broadcasted_iota(jnp.int32, (W, W), 1)
    same_head = ri // HEAD_DIM == rj // HEAD_DIM
    same_head_b = same_head.astype(BF16)
    incl_blk = same_head & (ri % C >= rj % C)
    strict_blk = same_head & (ri % C > rj % C)
    eye = (ri == rj).astype(F32)

    @pl.when(t == 0)
    def _():
        for h in range(N_HEADS):
            st_ref[C * h:C * (h + 1), :] = jnp.concatenate(
                [s0_ref[0, h] if g == h else jnp.zeros((C, HEAD_DIM), F32) for g in range(N_HEADS)], axis=1)
        xs_ref[0:8, :] = jnp.zeros((8, GDN_CONV_CH), F32)
        xs_ref[8 - tail:8, :] = buf_ref[0]

    @pl.when(t > 0)
    def _():
        xs_ref[0:8, :] = xs_ref[tb:tb + 8, :]

    xs_ref[8:, :] = x_ref[0]
    y = cw_ref[0:1, :] * xs_ref[8 - tail:8 - tail + tb, :]
    for i in range(1, CONV_WIDTH):
        y = y + cw_ref[i:i + 1, :] * xs_ref[8 - tail + i:8 - tail + i + tb, :]
    y = y * _sigmoid(y)
    q = y[:, 0:W]
    k = y[:, W:2 * W]
    qkv_ref[:, 0:W] = q * lax.rsqrt(_dot_sel(q * q, same_head_b) + EPS) * (HEAD_DIM ** -0.5)
    qkv_ref[:, W:2 * W] = k * lax.rsqrt(_dot_sel(k * k, same_head_b) + EPS)
    qkv_ref[:, 2 * W:3 * W] = y[:, 2 * W:3 * W]

    ci = lax.broadcasted_iota(jnp.int32, (C, C), 0)
    cj = lax.broadcasted_iota(jnp.int32, (C, C), 1)
    incl_b = (ci >= cj).astype(BF16)
    li = lax.broadcasted_iota(jnp.int32, (LANES, W), 0)
    lj = lax.broadcasted_iota(jnp.int32, (LANES, W), 1) // HEAD_DIM
    spread_g = (li == 4 + lj).astype(BF16)
    spread_beta = (li == 8 + lj).astype(BF16)
    token_is_lane = (lax.broadcasted_iota(jnp.int32, (C, W), 0)
                     == lax.broadcasted_iota(jnp.int32, (C, W), 1) % C).astype(F32)
    ones_rows = jnp.ones((W, C), BF16)

    def prepare(c):
        rows = pl.ds(c * C, C) if isinstance(c, int) else pl.ds(pl.multiple_of(c * C, C), C)
        sm = sm_ref[0, rows, :]
        gcum = _dot_sel(_dot_sel(sm, incl_b, x_first=False), spread_g)
        beta = _dot_sel(sm, spread_beta)
        qn = qkv_ref[rows, 0:W]
        kn = qkv_ref[rows, W:2 * W]
        vn = qkv_ref[rows, 2 * W:3 * W]
        g_of_col = _dot_sel(gcum * token_is_lane, ones_rows, x_first=False)
        g_of_row = jnp.concatenate([gcum] * N_HEADS, axis=0)
        decay = jnp.where(incl_blk, jnp.exp(jnp.where(incl_blk, g_of_row - g_of_col, 0.0)), 0.0)
        kb = kn * beta
        k16 = _head_blocks(kn).astype(BF16)
        m = jnp.where(strict_blk, _dot_nt(_head_blocks(kb).astype(BF16), k16) * decay, 0.0)
        pw = -m
        inv = eye + pw
        for _ in range(5):
            p2 = _split2(pw)
            pw = _mm3(p2, p2)
            inv = inv + _mm3(_split2(inv), _split2(pw))
        inv2 = _split2(inv)
        eg = jnp.exp(gcum)
        g_last = gcum[C - 1:C, :]
        u_ref[rows, :] = _fold_blocks(_mm3(inv2, _split2(_head_blocks(vn * beta))))
        w_ref[rows, :] = _fold_blocks(_mm3(inv2, _split2(_head_blocks(kb * eg))))
        qk_ref[rows, :] = _fold_blocks(
            jnp.where(incl_blk, _dot_nt(_head_blocks(qn).astype(BF16), k16) * decay, 0.0))
        qe_ref[rows, :] = qn * eg
        ke_ref[rows, :] = kn * jnp.exp(g_last - gcum)
        gl_rows = pl.ds(c * 8, 8) if isinstance(c, int) else pl.ds(pl.multiple_of(c * 8, 8), 8)
        gl_ref[gl_rows, :] = jnp.broadcast_to(jnp.exp(g_last), (8, W))

    group = 4 if n_chunks % 4 == 0 else 1
    if n_chunks == group or group == 1 and n_chunks < 4:
        for c in range(n_chunks):
            prepare(c)
    else:
        def prep_group(i, carry):
            for r in range(group):
                prepare(group * i + r)
            return carry
        lax.fori_loop(0, n_chunks // group, prep_group, 0)

    def scan(c, S):
        rows = pl.ds(c * C, C) if isinstance(c, int) else pl.ds(pl.multiple_of(c * C, C), C)
        gl_rows = pl.ds(c * 8, 8) if isinstance(c, int) else pl.ds(pl.multiple_of(c * 8, 8), 8)
        S16 = S.astype(BF16)
        v_new = u_ref[rows, :] - _dot(w_ref[rows, :].astype(BF16), S16)
        v16 = v_new.astype(BF16)
        o = _dot(qe_ref[rows, :].astype(BF16), S16) + _dot(qk_ref[rows, :].astype(BF16), _head_blocks(v16))
        S = S * gl_ref[gl_rows, :][0:1, :] + jnp.where(
            same_head, lax.dot_general(ke_ref[rows, :].astype(BF16), v16, _TN, preferred_element_type=F32), 0.0)
        ms = _dot_sel(o * o, same_head_b) * (1.0 / HEAD_DIM)
        zn = z_ref[0, rows, :]
        o_ref[0, rows, :] = (o * lax.rsqrt(ms + EPS) * ng_ref[...] * (zn * _sigmoid(zn))).astype(o_ref.dtype)
        return S

    S = st_ref[...]
    if n_chunks == 1:
        S = scan(0, S)
    else:
        S = lax.fori_loop(0, n_chunks, scan, S)
    st_ref[...] = S

    @pl.when(t == pl.num_programs(1) - 1)
    def _():
        for h in range(N_HEADS):
            sout_ref[0, h] = S[C * h:C * (h + 1), HEAD_DIM * h:HEAD_DIM * (h + 1)]
        cout_ref[0] = xs_ref[8 + tb - tail:8 + tb, :]


def _gdn(x, z, small, conv_w, buf, s0, norm_g, tb):
    B, T, _ = x.shape
    assert T % tb == 0 and tb % CHUNK == 0 and tb >= 8
    W = N_HEADS * HEAD_DIM
    tok = lambda w: pl.BlockSpec((1, tb, w), lambda b, t: (b, t, 0))
    per_b = lambda s: pl.BlockSpec((1,) + s, lambda b, t: (b,) + (0,) * len(s))
    natural = pltpu.VMEM((tb, W), F32)
    return pl.pallas_call(
        functools.partial(_gdn_kernel, tb=tb),
        grid=(B, T // tb),
        in_specs=[tok(768), tok(256), tok(128),
                  pl.BlockSpec((CONV_WIDTH, 768), lambda b, t: (0, 0)),
                  per_b((CONV_WIDTH - 1, 768)), per_b((N_HEADS, 64, 64)),
                  pl.BlockSpec((1, W), lambda b, t: (0, 0))],
        out_specs=[tok(256), per_b((N_HEADS, 64, 64)), per_b((CONV_WIDTH - 1, 768))],
        out_shape=[jax.ShapeDtypeStruct((B, T, 256), BF16),
                   jax.ShapeDtypeStruct((B, N_HEADS, 64, 64), F32),
                   jax.ShapeDtypeStruct((B, CONV_WIDTH - 1, 768), F32)],
        scratch_shapes=[pltpu.VMEM((tb + 8, 768), F32), pltpu.VMEM((tb, 768), F32)] + [natural] * 5
                       + [pltpu.VMEM((tb // CHUNK * 8, W), F32), pltpu.VMEM((W, W), F32)],
        compiler_params=_params(("parallel", "arbitrary")),
        name="gated_deltanet",
    )(x, z, small, conv_w, buf, s0, norm_g)


def _merge_kernel(x_ref, g1_ref, g2_ref, of_ref, od_ref, os_ref, og_ref, wgate_ref, wb_ref, wo_ref,
                  x1_ref, h_ref):
    x = x_ref[...]
    u = _rms(x, g1_ref[...]).astype(BF16)
    merged = None
    for i, o_ref in enumerate((of_ref, od_ref, os_ref, og_ref)):
        gate = _sigmoid(_dot(u, wgate_ref[:, i * D_MODEL:(i + 1) * D_MODEL]))
        term = gate * _dot(o_ref[...], wb_ref[i])
        merged = term if merged is None else merged + term
    x1 = x + _dot(merged.astype(BF16), wo_ref[...])
    x1_ref[...] = x1
    h_ref[...] = _rms(x1, g2_ref[...]).astype(BF16)


def _merge(x2, g1, g2, o_fox, o_diff, o_sb, o_gdn, w_gate, w_branch, w_out, tm):
    M = x2.shape[0]
    row = lambda w: pl.BlockSpec((tm, w), lambda i: (i, 0))
    return pl.pallas_call(
        _merge_kernel,
        grid=(M // tm,),
        in_specs=[row(D_MODEL), _const_spec((1, D_MODEL)), _const_spec((1, D_MODEL)),
                  row(256), row(256), row(256), row(256),
                  _const_spec((D_MODEL, N_BRANCH * D_MODEL)), _const_spec((N_BRANCH, BRANCH_W, D_MODEL)),
                  _const_spec((D_MODEL, D_MODEL))],
        out_specs=[row(D_MODEL), row(D_MODEL)],
        out_shape=[jax.ShapeDtypeStruct((M, D_MODEL), F32), jax.ShapeDtypeStruct((M, D_MODEL), BF16)],
        compiler_params=_params(("parallel",)),
        name="branch_merge",
    )(x2, g1, g2, o_fox, o_diff, o_sb, o_gdn, w_gate, w_branch, w_out)


def _ffn_kernel(x1_ref, h_ref, wg_ref, wu_ref, wd_ref, gf_ref, o_ref, *, final):
    h = h_ref[...]
    acc = x1_ref[...]
    half = D_FF // 2
    for f0 in (0, half):
        a = _dot(h, wg_ref[:, f0:f0 + half])
        b = _dot(h, wu_ref[:, f0:f0 + half])
        acc = acc + _dot((a * _sigmoid(a) * b).astype(BF16), wd_ref[f0:f0 + half, :])
    if final:
        acc = _rms(acc, gf_ref[...])
    o_ref[...] = acc


def _ffn(x1, h, wg, wu, wd, g_final, final, tm):
    M = x1.shape[0]
    row = pl.BlockSpec((tm, D_MODEL), lambda i: (i, 0))
    return pl.pallas_call(
        functools.partial(_ffn_kernel, final=final),
        grid=(M // tm,),
        in_specs=[row, row, _const_spec((D_MODEL, D_FF)), _const_spec((D_MODEL, D_FF)),
                  _const_spec((D_FF, D_MODEL)), _const_spec((1, D_MODEL))],
        out_specs=row,
        out_shape=jax.ShapeDtypeStruct((M, D_MODEL), F32),
        compiler_params=_params(("parallel",)),
        name="swiglu",
    )(x1, h, wg, wu, wd, g_final)


def _prep_layer(p, l):
    w = p["w_in"][l]
    cols = lambda a, b: w[:, a:b]
    small = jnp.concatenate([cols(_FOX0 + 768, _FOX0 + 772), cols(_GDN0 + 768, _GDN0 + 776),
                             jnp.zeros((D_MODEL, LANES - 12), F32)], axis=1)
    w_r = jnp.concatenate([cols(_FOX0, _FOX0 + 768), cols(_DIF0, _DIF0 + 768), cols(_SB0, _SB0 + 768),
                           cols(_GDN0, _GDN0 + 768), cols(_GDN0 + 776, _GDN0 + 1032), small], axis=1).astype(BF16)
    sp = jnp.zeros((8, LANES), F32)
    sp = sp.at[0, 0:4].set(p["b_fox_f"][l]).at[0, 4:8].set(p["gdn_dt_bias"][l]).at[1, 4:8].set(p["gdn_a_log"][l])
    return dict(
        w_r=w_r, sp=sp, w_gate=cols(_GATE0, _GATE0 + N_BRANCH * D_MODEL).astype(BF16),
        g1=p["norm1_g"][l].reshape(1, D_MODEL), g2=p["norm2_g"][l].reshape(1, D_MODEL),
        dl=jnp.stack([p["diff_lq1"][l], p["diff_lk1"][l], p["diff_lq2"][l], p["diff_lk2"][l]]),
        subln=jnp.tile(p["diff_subln_g"][l], 2).reshape(1, LANES),
        conv_w=p["gdn_conv_w"][l], norm_g=jnp.tile(p["gdn_norm_g"][l], N_HEADS).reshape(1, N_HEADS * HEAD_DIM),
        w_branch=p["w_branch"][l].astype(BF16), w_out=p["w_out"][l].astype(BF16),
        wg=p["w_ffn_gate"][l].astype(BF16), wu=p["w_ffn_up"][l].astype(BF16), wd=p["w_ffn_down"][l].astype(BF16),
    )


def _with_past(past, new16, t_pad):
    B, T, W = new16.shape
    parts = [new16] if past is None else [past.reshape(B, -1, W).astype(BF16), new16]
    n = sum(a.shape[1] for a in parts)
    if t_pad > n:
        parts.append(jnp.zeros((B, t_pad - n, W), BF16))
    return jnp.concatenate(parts, axis=1) if len(parts) > 1 else parts[0]


def _with_forget_lanes(x16, c3, query):
    B, T, _ = x16.shape
    pieces = jnp.stack([jnp.swapaxes(a if query else -a, 1, 2) for a in c3], axis=-1)
    ones = jnp.ones((B, T, N_HEADS, 3), BF16)
    zeros = jnp.zeros((B, T, N_HEADS, HEAD_DIM - 6), BF16)
    extra = [pieces, ones, zeros] if query else [ones, pieces, zeros]
    return jnp.concatenate([x16.reshape(B, T, N_HEADS, HEAD_DIM)] + extra, axis=-1).reshape(B, T, N_HEADS * LANES)


def _with_ones_lanes(v16):
    B, T, _ = v16.shape
    ones = jnp.ones((B, T, N_HEADS, HEAD_DIM), BF16)
    return jnp.concatenate([v16.reshape(B, T, N_HEADS, HEAD_DIM), ones], axis=-1).reshape(B, T, N_HEADS * LANES)


def _trunk(x, caches, p, layers, *, tm, tq, tk, sw, tb):
    B, T, _ = x.shape
    M = B * T
    past = 0 if caches is None else caches[0].shape[2]
    t_keys = -(-(past + T) // tk) * tk
    x2 = x.reshape(M, D_MODEL)
    new_state = []
    for l in range(DEPTH):
        lp = layers[l]
        c = None if caches is None else tuple(a[l] for a in caches)
        (fq, fk32, fk16, fv32, fv16, dq, dk32, dk16, dv32, dv16, sq, sk32, sk16, sv32, sv16,
         gqkv, gz, small) = _in_proj(x2, lp["g1"], lp["w_r"], lp["sp"], tm)
        b3 = lambda a: a.reshape(B, T, a.shape[-1])
        kv = lambda i, a: _with_past(None if c is None else c[i], b3(a), t_keys)

        logf = b3(small)[:, :, 0:N_HEADS]
        lf_all = logf if c is None else jnp.concatenate([c[2], logf], axis=1)
        r_pad = -(-t_keys // (16 * LANES)) * 16
        lf_all = jnp.pad(lf_all, ((0, 0), (0, r_pad * LANES - lf_all.shape[1]), (0, 0)))
        cum3 = [a.reshape(B, N_HEADS, -1)
                for a in _cumsum_time(jnp.swapaxes(lf_all, 1, 2).reshape(B, N_HEADS, r_pad, LANES))]
        fq_c = _with_forget_lanes(b3(fq), [a[:, :, past:past + T] for a in cum3], True)
        fk_c = _with_forget_lanes(kv(0, fk16), [a[:, :, :t_keys] for a in cum3], False)
        o_fox = _fox_attention(fq_c, fk_c, _with_ones_lanes(kv(1, fv16)), tq=tq, tk=tk, past=past)

        lam_init = 0.8 - 0.6 * math.exp(-0.3 * l)
        o_diff = _diff_attention(lp["dl"], lp["subln"], b3(dq), kv(3, dk16), _with_ones_lanes(kv(4, dv16)),
                                 tq=tq, tk=tk, past=past, lam_init=lam_init)

        o_sb = _sb_attention(b3(sq), kv(5, sk16), kv(6, sv16), tq=tq, tk=tk, sw=sw, past=past)

        buf = jnp.zeros((B, CONV_WIDTH - 1, GDN_CONV_CH), F32) if c is None else c[8]
        s0 = jnp.zeros((B, N_HEADS, 64, 64), F32) if c is None else c[7]
        o_gdn, s_new, conv_state = _gdn(b3(gqkv), b3(gz), b3(small), lp["conv_w"], buf, s0, lp["norm_g"], tb)

        x1, h = _merge(x2, lp["g1"], lp["g2"], o_fox.reshape(M, 256), o_diff.reshape(M, 256),
                       o_sb.reshape(M, 256), o_gdn.reshape(M, 256), lp["w_gate"], lp["w_branch"], lp["w_out"], tm)
        x2 = _ffn(x1, h, lp["wg"], lp["wu"], lp["wd"], p["final_norm_g"].reshape(1, D_MODEL),
                  l == DEPTH - 1, tm)
        h4 = lambda a: a.reshape(B, T, N_HEADS, HEAD_DIM)
        new_state.append((h4(fk32), h4(fv32), logf, h4(dk32), h4(dv32), h4(sk32), h4(sv32), s_new, conv_state))
    y = x2.reshape(B, T, D_MODEL)
    return y, tuple(jnp.stack(parts) for parts in zip(*new_state))


def kernel(x_prompt, x_sample, cache_fox_k, cache_fox_v, cache_fox_logf, cache_diff_k, cache_diff_v, cache_sb_k, cache_sb_v, state_gdn, state_gdn_conv, norm1_g, w_in, b_fox_f, diff_lq1, diff_lk1, diff_lq2, diff_lk2, diff_subln_g, gdn_conv_w, gdn_a_log, gdn_dt_bias, gdn_norm_g, w_branch, w_out, norm2_g, w_ffn_gate, w_ffn_up, w_ffn_down, final_norm_g):
    p = dict(norm1_g=norm1_g, w_in=w_in, b_fox_f=b_fox_f, diff_lq1=diff_lq1, diff_lk1=diff_lk1, diff_lq2=diff_lq2,
             diff_lk2=diff_lk2, diff_subln_g=diff_subln_g, gdn_conv_w=gdn_conv_w, gdn_a_log=gdn_a_log,
             gdn_dt_bias=gdn_dt_bias, gdn_norm_g=gdn_norm_g, w_branch=w_branch, w_out=w_out, norm2_g=norm2_g,
             w_ffn_gate=w_ffn_gate, w_ffn_up=w_ffn_up, w_ffn_down=w_ffn_down, final_norm_g=final_norm_g)
    layers = [_prep_layer(p, l) for l in range(DEPTH)]
    t_p = x_prompt.shape[1]
    blk = min(512, t_p)
    y_prompt, sp = _trunk(x_prompt, None, p, layers, tm=blk, tq=blk, tk=blk, sw=min(256, blk), tb=blk)
    caches = (cache_fox_k, cache_fox_v, cache_fox_logf, cache_diff_k, cache_diff_v, cache_sb_k, cache_sb_v,
              state_gdn, state_gdn_conv)
    b_s, t_s = x_sample.shape[0], x_sample.shape[1]
    keys_s = -(-(cache_fox_k.shape[2] + t_s) // LANES) * LANES
    y_sample, ss = _trunk(x_sample, caches, p, layers, tm=min(512, b_s * t_s), tq=t_s, tk=keys_s, sw=LANES, tb=t_s)
    return (y_prompt, y_sample) + sp + ss
```

```python
import functools
import math

import jax
import jax.numpy as jnp
from jax import lax
from jax.experimental import pallas as pl
from jax.experimental.pallas import tpu as pltpu

F32 = jnp.float32
BF16 = jnp.bfloat16

D_MODEL = 1024
DEPTH = 2
CHUNK = 64
HEAD_DIM = 64
N_HEADS = 4
DIFF_DK = 32
CONV_WIDTH = 4
BRANCH_W = 256
N_BRANCH = 4
D_FF = 2816
EPS = 1e-6
GDN_CONV_CH = 768

LANES = 128
NEG_BIG = -1e30
VMEM_LIMIT = 56 * 1024 * 1024
LOG2E = 1.4426950408889634
SB_CUTOFF = 105.0
FOX_CUTOFF = -160.0

_FOX0, _DIF0, _SB0, _GDN0, _GATE0 = 0, 772, 1540, 2308, 3340
N_PROJ = 3456

_NN = (((1,), (0,)), ((), ()))
_NT = (((1,), (1,)), ((), ()))
_TN = (((0,), (0,)), ((), ()))


def _dot(a, b):
    return jnp.dot(a, b, preferred_element_type=F32)


def _dot_nt(a, b):
    return lax.dot_general(a, b, _NT, preferred_element_type=F32)


def _split2(a):
    hi = a.astype(BF16)
    return hi, (a - hi.astype(F32)).astype(BF16)


def _dot_sel(x, sel, x_first=True):
    hi = x.astype(BF16)
    r = x - hi.astype(F32)
    mid = r.astype(BF16)
    lo = (r - mid.astype(F32)).astype(BF16)
    if x_first:
        return _dot(hi, sel) + (_dot(mid, sel) + _dot(lo, sel))
    return _dot(sel, hi) + (_dot(sel, mid) + _dot(sel, lo))


def _rms(x, g):
    return x * lax.rsqrt(jnp.mean(x * x, axis=-1, keepdims=True) + EPS) * g


def _softplus(t):
    return jnp.maximum(t, 0.0) + jnp.log1p(jnp.exp(-jnp.abs(t)))


def _sigmoid(t):
    return 1.0 / (1.0 + jnp.exp(-t))


def _const_spec(shape):
    nd = len(shape)
    return pl.BlockSpec(shape, lambda *_: (0,) * nd, pipeline_mode=pl.Buffered(1))


def _params(sem):
    return pltpu.CompilerParams(dimension_semantics=sem, vmem_limit_bytes=VMEM_LIMIT)


def _in_proj_kernel(x_ref, g_ref, w_ref, sp_ref,
                    fq_ref, fk32_ref, fk16_ref, fv32_ref, fv16_ref,
                    dq_ref, dk32_ref, dk16_ref, dv32_ref, dv16_ref,
                    sq_ref, sk32_ref, sk16_ref, sv32_ref, sv16_ref,
                    gqkv_ref, gz_ref, small_ref):
    u = _rms(x_ref[...], g_ref[...]).astype(BF16)

    def mm(c0, w):
        return _dot(u, w_ref[:, c0:c0 + w])

    def qkv(c0, q_scale, q_ref, k32, k16, v32, v16):
        q_ref[...] = (mm(c0, 256) * q_scale).astype(BF16)
        k = mm(c0 + 256, 256)
        k32[...] = k
        k16[...] = k.astype(BF16)
        v = mm(c0 + 512, 256)
        v32[...] = v
        v16[...] = v.astype(BF16)

    qkv(0, HEAD_DIM ** -0.5 * LOG2E, fq_ref, fk32_ref, fk16_ref, fv32_ref, fv16_ref)
    qkv(768, DIFF_DK ** -0.5 * LOG2E, dq_ref, dk32_ref, dk16_ref, dv32_ref, dv16_ref)
    qkv(1536, HEAD_DIM ** -0.5, sq_ref, sk32_ref, sk16_ref, sv32_ref, sv16_ref)
    gqkv_ref[...] = mm(2304, 768)
    gz_ref[...] = mm(3072, 256)
    t = mm(3328, 128) + sp_ref[0:1, :]
    lane = lax.broadcasted_iota(jnp.int32, (1, LANES), 1)
    sp = _softplus(t)
    logf = t - sp
    g = -jnp.exp(sp_ref[1:2, :]) * sp
    beta = _sigmoid(t)
    small_ref[...] = jnp.where(lane < 4, logf, jnp.where(lane < 8, g, jnp.where(lane < 12, beta, 0.0)))


def _in_proj(x2, g1, w_r, sp, tm):
    M = x2.shape[0]
    row = lambda w: pl.BlockSpec((tm, w), lambda i: (i, 0))
    f32o = lambda w: jax.ShapeDtypeStruct((M, w), F32)
    b16o = lambda w: jax.ShapeDtypeStruct((M, w), BF16)
    qkv_shapes = [b16o(256), f32o(256), b16o(256), f32o(256), b16o(256)]
    out_shape = qkv_shapes * 3 + [f32o(768), f32o(256), f32o(128)]
    out_specs = [row(256)] * 15 + [row(768), row(256), row(128)]
    return pl.pallas_call(
        _in_proj_kernel,
        grid=(M // tm,),
        in_specs=[row(D_MODEL), _const_spec((1, D_MODEL)), _const_spec((D_MODEL, N_PROJ)), _const_spec((8, LANES))],
        out_specs=out_specs,
        out_shape=out_shape,
        compiler_params=_params(("parallel",)),
        name="in_proj",
    )(x2, g1, w_r, sp)


def _cumsum_kernel(x_ref, hi_ref, mid_ref, lo_ref):
    R = x_ref.shape[2]
    ii = lax.broadcasted_iota(jnp.int32, (LANES, LANES), 0)
    jj = lax.broadcasted_iota(jnp.int32, (LANES, LANES), 1)
    upper = (ii <= jj).astype(BF16)
    ri = lax.broadcasted_iota(jnp.int32, (R, R), 0)
    rj = lax.broadcasted_iota(jnp.int32, (R, R), 1)
    lower = (rj < ri).astype(BF16)
    for h in range(N_HEADS):
        within = _dot_sel(x_ref[0, h], upper)
        tot = jnp.broadcast_to(within[:, LANES - 1:LANES], (R, LANES))
        c = (within + _dot_sel(tot, lower, x_first=False)) * LOG2E
        hi = c.astype(BF16).astype(F32)
        r = c - hi
        mid = r.astype(BF16).astype(F32)
        hi_ref[0, h] = hi
        mid_ref[0, h] = mid
        lo_ref[0, h] = (r - mid).astype(BF16).astype(F32)


def _cumsum_time(x4):
    B, H, R, _ = x4.shape
    spec = pl.BlockSpec((1, H, R, LANES), lambda b: (b, 0, 0, 0))
    return pl.pallas_call(
        _cumsum_kernel, grid=(B,), in_specs=[spec], out_specs=[spec] * 3,
        out_shape=[jax.ShapeDtypeStruct(x4.shape, F32)] * 3,
        compiler_params=_params(("parallel",)), name="fox_cumsum",
    )(x4)


def _head_pack_kernel(*refs, kind, tb, sb, off0):
    x_ref = refs[0]
    o_ref = refs[4] if kind == "k" else refs[-1]
    src = lax.broadcasted_iota(jnp.int32, (2 * LANES, 4 * LANES), 0)
    dst = lax.broadcasted_iota(jnp.int32, (2 * LANES, 4 * LANES), 1)
    place = ((dst // LANES == src // HEAD_DIM) & (dst % LANES == src % HEAD_DIM)).astype(BF16)
    lane = lax.broadcasted_iota(jnp.int32, (1, 4 * LANES), 1) % LANES
    if kind == "v":
        ones = lane >= HEAD_DIM
    else:
        c_lane, one_lane = (HEAD_DIM, HEAD_DIM + 3) if kind == "q" else (HEAD_DIM + 3, HEAD_DIM)
        ones = (lane >= one_lane) & (lane < one_lane + 3)
        pj = lax.broadcasted_iota(jnp.int32, (16, 4 * LANES), 0)
        pd = lax.broadcasted_iota(jnp.int32, (16, 4 * LANES), 1)
        hit = (pj < 12) & (pd == LANES * (pj % 4) + c_lane + pj // 4)
        spread = jnp.where(hit, 1.0 if kind == "q" else -1.0, 0.0).astype(BF16)
        ti = lax.broadcasted_iota(jnp.int32, (sb, sb), 0)
        tj = lax.broadcasted_iota(jnp.int32, (sb, sb), 1)
        eye = (ti == tj).astype(BF16)
    if kind == "k":
        hs_i = lax.broadcasted_iota(jnp.int32, (2 * LANES, LANES), 0) // HEAD_DIM
        hs_j = lax.broadcasted_iota(jnp.int32, (2 * LANES, LANES), 1)
        head_sum = (hs_i == hs_j).astype(BF16)
        norm2 = jnp.zeros((1, LANES), F32)
    for s in range(tb // sb):
        out = _dot(x_ref[0, s * sb:(s + 1) * sb, :], place)
        if kind == "k":
            xf = x_ref[0, s * sb:(s + 1) * sb, :].astype(F32)
            norm2 = jnp.maximum(norm2, jnp.max(_dot((xf * xf).astype(BF16), head_sum), axis=0, keepdims=True))
        if kind != "v":
            off = off0 + s * sb
            row = off // LANES
            if tb % LANES == 0:
                row = row + pl.program_id(1) * (tb // LANES)
            lanes = slice(off % LANES, off % LANES + sb)
            pieces = [c_ref[0, h, pl.ds(row, 1), lanes] for c_ref in refs[1:4] for h in range(N_HEADS)]
            pieces = jnp.concatenate(pieces + [jnp.zeros((4, sb), F32)], axis=0).astype(BF16)
            columns = _dot_nt(eye, pieces).astype(BF16)
            out = out + _dot(columns, spread)
        o_ref[0, s * sb:(s + 1) * sb, :] = jnp.where(ones, 1.0, out).astype(o_ref.dtype)
    if kind == "k":
        refs[5][0, 0] = jnp.broadcast_to(norm2, (8, LANES))


def _head_pack(x16, c3, kind, off0=0):
    B, T, _ = x16.shape
    tb = 512 if T % 512 == 0 else T
    sb = min(LANES, tb)
    assert tb % sb == 0 and (tb == T or (tb % LANES == 0 and off0 % LANES == 0))
    assert kind == "v" or off0 % LANES + sb <= LANES
    specs = [pl.BlockSpec((1, tb, 2 * LANES), lambda b, t: (b, t, 0))]
    args = [x16]
    if kind != "v":
        R = c3[0].shape[2]
        specs += [pl.BlockSpec((1, N_HEADS, R, LANES), lambda b, t: (b, 0, 0, 0))] * 3
        args += list(c3)
    out_specs = [pl.BlockSpec((1, tb, 4 * LANES), lambda b, t: (b, t, 0))]
    out_shape = [jax.ShapeDtypeStruct((B, T, 4 * LANES), BF16)]
    if kind == "k":
        out_specs.append(pl.BlockSpec((1, 1, 8, LANES), lambda b, t: (b, t, 0, 0)))
        out_shape.append(jax.ShapeDtypeStruct((B, T // tb, 8, LANES), F32))
    out = pl.pallas_call(
        functools.partial(_head_pack_kernel, kind=kind, tb=tb, sb=sb, off0=off0),
        grid=(B, T // tb),
        in_specs=specs,
        out_specs=out_specs,
        out_shape=out_shape,
        compiler_params=_params(("parallel", "parallel")),
        name="head_pack_" + kind,
    )(*args)
    return out if kind == "k" else out[0]


def _lane_group(q, lo, hi):
    lane = lax.broadcasted_iota(jnp.int32, (1, LANES), 1)
    return jnp.where((lane >= lo) & (lane < hi), q, jnp.zeros_like(q))


def _chunk_plan(qi, *, tq, tk, past, single):
    q0 = past if single else past + qi * tq
    return q0 // tk, q0


def _for_chunks(n_full, n_diag, step):
    if isinstance(n_full, int):
        for j in range(n_full):
            step(j, False)
    else:
        def body(j, c):
            step(j, False)
            return c
        lax.fori_loop(0, n_full, body, 0)
    for r in range(n_diag):
        step(n_full + r, True)


def _rows(j, tk):
    return pl.ds(j * tk, tk) if isinstance(j, int) else pl.ds(pl.multiple_of(j * tk, tk), tk)


def _lane_tile(x, reps):
    return x if reps == 1 else jnp.concatenate([x] * reps, axis=1)


def _softmax_init(m_ref, acc_ref):
    m_ref[...] = jnp.full(m_ref.shape, NEG_BIG, F32)
    acc_ref[...] = jnp.zeros(acc_ref.shape, F32)


def _softmax_update(s, v1, m_ref, acc_ref, c):
    m_old = m_ref[c]
    m_new = jnp.maximum(m_old, jnp.max(s, axis=1, keepdims=True))
    p = jnp.exp2(s - _lane_tile(m_new, s.shape[1] // LANES))
    acc_ref[c] = jnp.exp2(m_old - m_new) * acc_ref[c] + _dot(p.astype(BF16), v1)
    m_ref[c] = m_new


def _normalised(acc):
    return acc / pltpu.roll(acc, HEAD_DIM, axis=1)


def _pair_lanes(even, odd):
    lane = lax.broadcasted_iota(jnp.int32, (1, LANES), 1)
    return jnp.where(lane < HEAD_DIM, even, pltpu.roll(odd, HEAD_DIM, axis=1))


def _fox_kernel(q_ref, k_ref, v_ref, *rest, tq, tk, past, single):
    bound_ref = None if single else rest[0]
    o_ref, m_ref, acc_ref = rest[-3:]
    n_full, q0 = _chunk_plan(pl.program_id(1), tq=tq, tk=tk, past=past, single=single)
    q_pos = q0 + lax.broadcasted_iota(jnp.int32, (tq, 1), 0)
    _softmax_init(m_ref, acc_ref)

    def step(j, masked):
        rows = _rows(j, tk)
        if masked:
            visible = j * tk + lax.broadcasted_iota(jnp.int32, (1, tk), 1) <= q_pos
        for h in range(N_HEADS):
            head = slice(LANES * h, LANES * (h + 1))
            s = _dot_nt(q_ref[0, :, head], k_ref[0, rows, head])
            if masked:
                s = jnp.where(visible, s, NEG_BIG)
            _softmax_update(s, v_ref[0, rows, head], m_ref, acc_ref, h)

    n_diag = tq // tk if tq > tk else 1
    if single:
        _for_chunks(n_full, n_diag, step)
    else:
        for r in range(n_diag):
            step(n_full + r, True)
        lane = lax.broadcasted_iota(jnp.int32, (1, LANES), 1)
        need = None
        for h in range(N_HEADS):
            qh = q_ref[0, :, LANES * h:LANES * (h + 1)].astype(F32)
            q_norm = jnp.sqrt(jnp.max(jnp.sum(jnp.where(lane < HEAD_DIM, qh * qh, 0.0), axis=1, keepdims=True),
                                      axis=0, keepdims=True))
            c_q = jnp.sum(jnp.where((lane >= HEAD_DIM) & (lane < HEAD_DIM + 3), qh, 0.0), axis=1, keepdims=True)
            slack = jnp.max(c_q - m_ref[h][:, 0:1], axis=0, keepdims=True)
            kb = bound_ref[0, h]
            need_h = q_norm * kb[0:1, :] + slack - kb[1:2, :] >= FOX_CUTOFF
            need = need_h if need is None else need | need_h
        first = jnp.min(jnp.where(need & (lane < n_full), lane, n_full))

        def body(j, c):
            step(j, False)
            return c
        lax.fori_loop(first, n_full, body, 0)
    for pr in range(2):
        o_ref[0, :, LANES * pr:LANES * (pr + 1)] = _pair_lanes(
            _normalised(acc_ref[2 * pr]), _normalised(acc_ref[2 * pr + 1])).astype(o_ref.dtype)


def _diff_kernel(dl_ref, g_ref, q_ref, k_ref, v_ref, o_ref, m_ref, acc_ref, *, tq, tk, past, single, lam_init):
    n_full, q0 = _chunk_plan(pl.program_id(2), tq=tq, tk=tk, past=past, single=single)
    q_chunk = (q0 + lax.broadcasted_iota(jnp.int32, (tq, 1), 0)) // CHUNK
    q = q_ref[0]
    qg = [_lane_group(q, DIFF_DK * g, DIFF_DK * (g + 1)) for g in range(4)]
    _softmax_init(m_ref, acc_ref)

    def step(j, masked):
        rows = _rows(j, tk)
        k = k_ref[0, rows, :]
        if masked:
            visible = (j * tk + lax.broadcasted_iota(jnp.int32, (1, tk), 1)) // CHUNK <= q_chunk
        for g in range(4):
            s = _dot_nt(qg[g], k)
            if masked:
                s = jnp.where(visible, s, NEG_BIG)
            _softmax_update(s, v_ref[0, rows, LANES * (g // 2):LANES * (g // 2 + 1)], m_ref, acc_ref, g)

    _for_chunks(n_full, tq // tk if tq > tk else 1, step)
    dl = dl_ref[...]
    lam = (jnp.exp(jnp.sum(dl[0:1] * dl[1:2], axis=1, keepdims=True))
           - jnp.exp(jnp.sum(dl[2:3] * dl[3:4], axis=1, keepdims=True)) + lam_init)
    p = [_normalised(acc_ref[g]) for g in range(4)]
    lane = lax.broadcasted_iota(jnp.int32, (1, LANES), 1)
    first = lane < 64
    o = _pair_lanes(p[0] - lam * p[1], p[2] - lam * p[3])
    sq = o * o
    ms = jnp.where(first,
                   jnp.sum(jnp.where(first, sq, 0.0), axis=1, keepdims=True),
                   jnp.sum(jnp.where(first, 0.0, sq), axis=1, keepdims=True)) * (1.0 / HEAD_DIM)
    o_ref[0] = (o * lax.rsqrt(ms + EPS) * g_ref[...] * (1.0 - lam_init)).astype(o_ref.dtype)


def _sb_kernel(q_ref, k_ref, v_ref, o_ref, right_ref, acc_ref, *, tq, tk, sw, past, single):
    n_full, q0 = _chunk_plan(pl.program_id(1), tq=tq, tk=tk, past=past, single=single)
    q_pos = q0 + lax.broadcasted_iota(jnp.int32, (tq, 1), 0)
    ki = lax.broadcasted_iota(jnp.int32, (2 * sw, sw), 0) % sw
    kj = lax.broadcasted_iota(jnp.int32, (2 * sw, sw), 1)
    after = (ki > kj).astype(BF16)
    right_ref[...] = jnp.zeros(right_ref.shape, F32)
    acc_ref[...] = jnp.zeros(acc_ref.shape, F32)
    n_diag = tq // tk if tq > tk else 1

    def step(j, masked):
        for sub in reversed(range(tk // sw)):
            use_mask = masked and not (single and j * tk + (sub + 1) * sw <= past)
            rows = (pl.ds(j * tk + sub * sw, sw) if isinstance(j, int)
                    else pl.ds(pl.multiple_of(j * tk + sub * sw, sw), sw))
            if use_mask:
                mask = j * tk + sub * sw + lax.broadcasted_iota(jnp.int32, (1, sw), 1) < q_pos
            for h in range(N_HEADS):
                pr = slice(LANES * (h // 2), LANES * (h // 2 + 1))
                qh = _lane_group(q_ref[0, :, pr], HEAD_DIM * (h % 2), HEAD_DIM * (h % 2 + 1))
                z = _dot_nt(qh, k_ref[0, rows, pr])
                sp = jnp.maximum(z, 0.0) + jnp.log(1.0 + jnp.exp2(jnp.abs(z) * -LOG2E))
                if use_mask:
                    sp = jnp.where(mask, sp, 0.0)
                hi, lo = _split2(sp)
                later = _dot(jnp.concatenate([hi, lo], axis=1), after)
                right = right_ref[h]
                a = jnp.exp(z - sp - later - _lane_tile(right, sw // LANES))
                if use_mask:
                    a = jnp.where(mask, a, 0.0)
                acc_ref[h] += _dot(a.astype(BF16), v_ref[0, rows, pr])
                right_ref[h] = right + jnp.sum(sp, axis=1, keepdims=True)

    for r in reversed(range(n_diag)):
        step(n_full + r, True)
    if isinstance(n_full, int):
        for j in reversed(range(n_full)):
            step(j, False)
    else:
        def more(c):
            return (c[0] < n_full) & c[1]

        def body(c):
            step(n_full - 1 - c[0], False)
            return c[0] + 1, jnp.min(right_ref[...]) < SB_CUTOFF
        lax.while_loop(more, body, (jnp.int32(0), jnp.min(right_ref[...]) < SB_CUTOFF))
    lane = lax.broadcasted_iota(jnp.int32, (1, LANES), 1)
    for pr in range(2):
        o_ref[0, :, LANES * pr:LANES * (pr + 1)] = jnp.where(
            lane < 64, acc_ref[2 * pr], acc_ref[2 * pr + 1]).astype(o_ref.dtype)


def _check_tiles(Tq, Tk, tq, tk, past):
    single = Tq == tq
    assert Tq % tq == 0 and tq % CHUNK == 0 and Tk % tk == 0 and past + Tq <= Tk
    if single:
        n_full = past // tk
        assert (n_full + max(tq // tk, 1)) * tk >= past + Tq
    else:
        assert tq % tk == 0 and past % tk == 0
    return single


def _fox_attention(q, k, v, bounds, *, tq, tk, past):
    B, Tq, _ = q.shape
    Tk = k.shape[1]
    single = _check_tiles(Tq, Tk, tq, tk, past)
    assert single or Tk // tk <= LANES
    keys = pl.BlockSpec((1, Tk, 4 * LANES), lambda b, i: (b, 0, 0), pipeline_mode=pl.Buffered(1))
    extra_specs = [] if single else [pl.BlockSpec((1, N_HEADS, 8, LANES), lambda b, i: (b, 0, 0, 0))]
    extra = [] if single else [bounds]
    return pl.pallas_call(
        functools.partial(_fox_kernel, tq=tq, tk=tk, past=past, single=single),
        grid=(B, Tq // tq),
        in_specs=[pl.BlockSpec((1, tq, 4 * LANES), lambda b, i: (b, i, 0)), keys, keys] + extra_specs,
        out_specs=pl.BlockSpec((1, tq, 2 * LANES), lambda b, i: (b, i, 0)),
        out_shape=jax.ShapeDtypeStruct((B, Tq, 2 * LANES), BF16),
        scratch_shapes=[pltpu.VMEM((N_HEADS, tq, LANES), F32), pltpu.VMEM((N_HEADS, tq, LANES), F32)],
        compiler_params=_params(("parallel", "arbitrary")),
        name="fox_attention",
    )(q, k, v, *extra)


def _diff_attention(dl, subln, q, k, v, *, tq, tk, past, lam_init):
    B, Tq, _ = q.shape
    Tk = k.shape[1]
    single = _check_tiles(Tq, Tk, tq, tk, past)
    blk = pl.BlockSpec((1, tq, LANES), lambda b, p, i: (b, i, p))
    keys = pl.BlockSpec((1, Tk, LANES), lambda b, p, i: (b, 0, p))
    values = pl.BlockSpec((1, Tk, 2 * LANES), lambda b, p, i: (b, 0, p))
    return pl.pallas_call(
        functools.partial(_diff_kernel, tq=tq, tk=tk, past=past, single=single, lam_init=lam_init),
        grid=(B, 2, Tq // tq),
        in_specs=[pl.BlockSpec((4, DIFF_DK), lambda b, p, i: (0, 0)),
                  pl.BlockSpec((1, LANES), lambda b, p, i: (0, 0)), blk, keys, values],
        out_specs=blk,
        out_shape=jax.ShapeDtypeStruct((B, Tq, 2 * LANES), BF16),
        scratch_shapes=[pltpu.VMEM((4, tq, LANES), F32), pltpu.VMEM((4, tq, LANES), F32)],
        compiler_params=_params(("parallel", "parallel", "arbitrary")),
        name="diff_attention",
    )(dl, subln, q, k, v)


def _sb_attention(q, k, v, *, tq, tk, sw, past):
    B, Tq, _ = q.shape
    Tk = k.shape[1]
    single = _check_tiles(Tq, Tk, tq, tk, past)
    assert tk % sw == 0
    kv = pl.BlockSpec((1, Tk, 2 * LANES), lambda b, i: (b, 0, 0))
    blk = pl.BlockSpec((1, tq, 2 * LANES), lambda b, i: (b, i, 0))
    return pl.pallas_call(
        functools.partial(_sb_kernel, tq=tq, tk=tk, sw=sw, past=past, single=single),
        grid=(B, Tq // tq),
        in_specs=[blk, kv, kv],
        out_specs=blk,
        out_shape=jax.ShapeDtypeStruct((B, Tq, 2 * LANES), BF16),
        scratch_shapes=[pltpu.VMEM((N_HEADS, tq, LANES), F32), pltpu.VMEM((N_HEADS, tq, LANES), F32)],
        compiler_params=_params(("parallel", "arbitrary")),
        name="sb_attention",
    )(q, k, v)


def _mm3(a2, b2, dims=_NN):
    (ah, al), (bh, bl) = a2, b2
    dg = lambda x, y: lax.dot_general(x, y, dims, preferred_element_type=F32)
    return dg(ah, bh) + (dg(ah, bl) + dg(al, bh))


def _head_blocks(x):
    lane_head = lax.broadcasted_iota(jnp.int32, (1, 256), 1) // HEAD_DIM
    return jnp.concatenate([jnp.where(lane_head == h, x, 0.0) for h in range(N_HEADS)], axis=0)


def _fold_blocks(x):
    C = CHUNK
    return (x[0:C] + x[C:2 * C]) + (x[2 * C:3 * C] + x[3 * C:4 * C])


def _gdn_kernel(x_ref, z_ref, sm_ref, cw_ref, buf_ref, s0_ref, ng_ref,
                o_ref, sout_ref, cout_ref,
                xs_ref, qkv_ref, u_ref, w_ref, qe_ref, ke_ref, qk_ref, gl_ref, st_ref, *, tb):
    t = pl.program_id(1)
    C = CHUNK
    W = N_HEADS * HEAD_DIM
    tail = CONV_WIDTH - 1
    n_chunks = tb // C

    ri = lax.broadcasted_iota(jnp.int32, (W, W), 0)
    rj = lax.broadcasted_iota(jnp.int32, (W, W), 1)
    same_head = ri // HEAD_DIM == rj // HEAD_DIM
    same_head_b = same_head.astype(BF16)
    incl_blk = same_head & (ri % C >= rj % C)
    strict_blk = same_head & (ri % C > rj % C)
    eye = (ri == rj).astype(F32)

    @pl.when(t == 0)
    def _():
        for h in range(N_HEADS):
            st_ref[C * h:C * (h + 1), :] = jnp.concatenate(
                [s0_ref[0, h] if g == h else jnp.zeros((C, HEAD_DIM), F32) for g in range(N_HEADS)], axis=1)
        xs_ref[0:8, :] = jnp.zeros((8, GDN_CONV_CH), F32)
        xs_ref[8 - tail:8, :] = buf_ref[0]

    @pl.when(t > 0)
    def _():
        xs_ref[0:8, :] = xs_ref[tb:tb + 8, :]

    xs_ref[8:, :] = x_ref[0]
    y = cw_ref[0:1, :] * xs_ref[8 - tail:8 - tail + tb, :]
    for i in range(1, CONV_WIDTH):
        y = y + cw_ref[i:i + 1, :] * xs_ref[8 - tail + i:8 - tail + i + tb, :]
    y = y * _sigmoid(y)
    q = y[:, 0:W]
    k = y[:, W:2 * W]
    qkv_ref[:, 0:W] = q * lax.rsqrt(_dot_sel(q * q, same_head_b) + EPS) * (HEAD_DIM ** -0.5)
    qkv_ref[:, W:2 * W] = k * lax.rsqrt(_dot_sel(k * k, same_head_b) + EPS)
    qkv_ref[:, 2 * W:3 * W] = y[:, 2 * W:3 * W]

    ci = lax.broadcasted_iota(jnp.int32, (C, C), 0)
    cj = lax.broadcasted_iota(jnp.int32, (C, C), 1)
    incl_b = (ci >= cj).astype(BF16)
    li = lax.broadcasted_iota(jnp.int32, (LANES, W), 0)
    lj = lax.broadcasted_iota(jnp.int32, (LANES, W), 1) // HEAD_DIM
    spread_g = (li == 4 + lj).astype(BF16)
    spread_beta = (li == 8 + lj).astype(BF16)
    token_is_lane = (lax.broadcasted_iota(jnp.int32, (C, W), 0)
                     == lax.broadcasted_iota(jnp.int32, (C, W), 1) % C).astype(F32)
    ones_rows = jnp.ones((W, C), BF16)

    def prepare(c):
        rows = pl.ds(c * C, C) if isinstance(c, int) else pl.ds(pl.multiple_of(c * C, C), C)
        sm = sm_ref[0, rows, :]
        gcum = _dot_sel(_dot_sel(sm, incl_b, x_first=False), spread_g)
        beta = _dot_sel(sm, spread_beta)
        qn = qkv_ref[rows, 0:W]
        kn = qkv_ref[rows, W:2 * W]
        vn = qkv_ref[rows, 2 * W:3 * W]
        g_of_col = _dot_sel(gcum * token_is_lane, ones_rows, x_first=False)
        g_of_row = jnp.concatenate([gcum] * N_HEADS, axis=0)
        decay = jnp.where(incl_blk, jnp.exp(jnp.where(incl_blk, g_of_row - g_of_col, 0.0)), 0.0)
        kb = kn * beta
        k16 = _head_blocks(kn).astype(BF16)
        m = jnp.where(strict_blk, _dot_nt(_head_blocks(kb).astype(BF16), k16) * decay, 0.0)
        pw = -m
        inv = eye + pw
        for _ in range(5):
            p2 = _split2(pw)
            pw = _mm3(p2, p2)
            inv = inv + _mm3(_split2(inv), _split2(pw))
        inv2 = _split2(inv)
        eg = jnp.exp(gcum)
        g_last = gcum[C - 1:C, :]
        u_ref[rows, :] = _fold_blocks(_mm3(inv2, _split2(_head_blocks(vn * beta))))
        w_ref[rows, :] = _fold_blocks(_mm3(inv2, _split2(_head_blocks(kb * eg))))
        qk_ref[rows, :] = _fold_blocks(
            jnp.where(incl_blk, _dot_nt(_head_blocks(qn).astype(BF16), k16) * decay, 0.0))
        qe_ref[rows, :] = qn * eg
        ke_ref[rows, :] = kn * jnp.exp(g_last - gcum)
        gl_rows = pl.ds(c * 8, 8) if isinstance(c, int) else pl.ds(pl.multiple_of(c * 8, 8), 8)
        gl_ref[gl_rows, :] = jnp.broadcast_to(jnp.exp(g_last), (8, W))

    group = 4 if n_chunks % 4 == 0 else 1
    if n_chunks == group or group == 1 and n_chunks < 4:
        for c in range(n_chunks):
            prepare(c)
    else:
        def prep_group(i, carry):
            for r in range(group):
                prepare(group * i + r)
            return carry
        lax.fori_loop(0, n_chunks // group, prep_group, 0)

    def scan(c, S):
        rows = pl.ds(c * C, C) if isinstance(c, int) else pl.ds(pl.multiple_of(c * C, C), C)
        gl_rows = pl.ds(c * 8, 8) if isinstance(c, int) else pl.ds(pl.multiple_of(c * 8, 8), 8)
        S16 = S.astype(BF16)
        v_new = u_ref[rows, :] - _dot(w_ref[rows, :].astype(BF16), S16)
        v16 = v_new.astype(BF16)
        o = _dot(qe_ref[rows, :].astype(BF16), S16) + _dot(qk_ref[rows, :].astype(BF16), _head_blocks(v16))
        S = S * gl_ref[gl_rows, :][0:1, :] + jnp.where(
            same_head, lax.dot_general(ke_ref[rows, :].astype(BF16), v16, _TN, preferred_element_type=F32), 0.0)
        ms = _dot_sel(o * o, same_head_b) * (1.0 / HEAD_DIM)
        zn = z_ref[0, rows, :]
        o_ref[0, rows, :] = (o * lax.rsqrt(ms + EPS) * ng_ref[...] * (zn * _sigmoid(zn))).astype(o_ref.dtype)
        return S

    S = st_ref[...]
    if n_chunks == 1:
        S = scan(0, S)
    else:
        S = lax.fori_loop(0, n_chunks, scan, S)
    st_ref[...] = S

    @pl.when(t == pl.num_programs(1) - 1)
    def _():
        for h in range(N_HEADS):
            sout_ref[0, h] = S[C * h:C * (h + 1), HEAD_DIM * h:HEAD_DIM * (h + 1)]
        cout_ref[0] = xs_ref[8 + tb - tail:8 + tb, :]


def _gdn(x, z, small, conv_w, buf, s0, norm_g, tb):
    B, T, _ = x.shape
    assert T % tb == 0 and tb % CHUNK == 0 and tb >= 8
    W = N_HEADS * HEAD_DIM
    tok = lambda w: pl.BlockSpec((1, tb, w), lambda b, t: (b, t, 0))
    per_b = lambda s: pl.BlockSpec((1,) + s, lambda b, t: (b,) + (0,) * len(s))
    natural = pltpu.VMEM((tb, W), F32)
    return pl.pallas_call(
        functools.partial(_gdn_kernel, tb=tb),
        grid=(B, T // tb),
        in_specs=[tok(768), tok(256), tok(128),
                  pl.BlockSpec((CONV_WIDTH, 768), lambda b, t: (0, 0)),
                  per_b((CONV_WIDTH - 1, 768)), per_b((N_HEADS, 64, 64)),
                  pl.BlockSpec((1, W), lambda b, t: (0, 0))],
        out_specs=[tok(256), per_b((N_HEADS, 64, 64)), per_b((CONV_WIDTH - 1, 768))],
        out_shape=[jax.ShapeDtypeStruct((B, T, 256), BF16),
                   jax.ShapeDtypeStruct((B, N_HEADS, 64, 64), F32),
                   jax.ShapeDtypeStruct((B, CONV_WIDTH - 1, 768), F32)],
        scratch_shapes=[pltpu.VMEM((tb + 8, 768), F32), pltpu.VMEM((tb, 768), F32)] + [natural] * 5
                       + [pltpu.VMEM((tb // CHUNK * 8, W), F32), pltpu.VMEM((W, W), F32)],
        compiler_params=_params(("parallel", "arbitrary")),
        name="gated_deltanet",
    )(x, z, small, conv_w, buf, s0, norm_g)


def _merge_kernel(x_ref, g1_ref, g2_ref, of_ref, od_ref, os_ref, og_ref, wgate_ref, wb_ref, wo_ref,
                  x1_ref, h_ref):
    x = x_ref[...]
    u = _rms(x, g1_ref[...]).astype(BF16)
    merged = None
    for i, o_ref in enumerate((of_ref, od_ref, os_ref, og_ref)):
        gate = _sigmoid(_dot(u, wgate_ref[:, i * D_MODEL:(i + 1) * D_MODEL]))
        term = gate * _dot(o_ref[...], wb_ref[i])
        merged = term if merged is None else merged + term
    x1 = x + _dot(merged.astype(BF16), wo_ref[...])
    x1_ref[...] = x1
    h_ref[...] = _rms(x1, g2_ref[...]).astype(BF16)


def _merge(x2, g1, g2, o_fox, o_diff, o_sb, o_gdn, w_gate, w_branch, w_out, tm):
    M = x2.shape[0]
    row = lambda w: pl.BlockSpec((tm, w), lambda i: (i, 0))
    return pl.pallas_call(
        _merge_kernel,
        grid=(M // tm,),
        in_specs=[row(D_MODEL), _const_spec((1, D_MODEL)), _const_spec((1, D_MODEL)),
                  row(256), row(256), row(256), row(256),
                  _const_spec((D_MODEL, N_BRANCH * D_MODEL)), _const_spec((N_BRANCH, BRANCH_W, D_MODEL)),
                  _const_spec((D_MODEL, D_MODEL))],
        out_specs=[row(D_MODEL), row(D_MODEL)],
        out_shape=[jax.ShapeDtypeStruct((M, D_MODEL), F32), jax.ShapeDtypeStruct((M, D_MODEL), BF16)],
        compiler_params=_params(("parallel",)),
        name="branch_merge",
    )(x2, g1, g2, o_fox, o_diff, o_sb, o_gdn, w_gate, w_branch, w_out)


def _ffn_kernel(x1_ref, h_ref, wg_ref, wu_ref, wd_ref, gf_ref, o_ref, *, final):
    h = h_ref[...]
    acc = x1_ref[...]
    half = D_FF // 2
    for f0 in (0, half):
        a = _dot(h, wg_ref[:, f0:f0 + half])
        b = _dot(h, wu_ref[:, f0:f0 + half])
        acc = acc + _dot((a * _sigmoid(a) * b).astype(BF16), wd_ref[f0:f0 + half, :])
    if final:
        acc = _rms(acc, gf_ref[...])
    o_ref[...] = acc


def _ffn(x1, h, wg, wu, wd, g_final, final, tm):
    M = x1.shape[0]
    row = pl.BlockSpec((tm, D_MODEL), lambda i: (i, 0))
    return pl.pallas_call(
        functools.partial(_ffn_kernel, final=final),
        grid=(M // tm,),
        in_specs=[row, row, _const_spec((D_MODEL, D_FF)), _const_spec((D_MODEL, D_FF)),
                  _const_spec((D_FF, D_MODEL)), _const_spec((1, D_MODEL))],
        out_specs=row,
        out_shape=jax.ShapeDtypeStruct((M, D_MODEL), F32),
        compiler_params=_params(("parallel",)),
        name="swiglu",
    )(x1, h, wg, wu, wd, g_final)


def _transpose_cast_kernel(x_ref, o_ref):
    o_ref[...] = x_ref[...].T.astype(o_ref.dtype)


def _transpose_cast(wt):
    n, k = wt.shape
    assert n % LANES == 0
    return pl.pallas_call(
        _transpose_cast_kernel, grid=(n // LANES,),
        in_specs=[pl.BlockSpec((LANES, k), lambda i: (i, 0))],
        out_specs=pl.BlockSpec((k, LANES), lambda i: (0, i)),
        out_shape=jax.ShapeDtypeStruct((k, n), BF16),
        compiler_params=_params(("parallel",)), name="transpose_cast",
    )(wt)


def _prep_layer(p, l):
    wt = jnp.swapaxes(p["w_in"][l], 0, 1)
    cols = lambda a, b: wt[a:b]
    small = jnp.concatenate([cols(_FOX0 + 768, _FOX0 + 772), cols(_GDN0 + 768, _GDN0 + 776),
                             jnp.zeros((LANES - 12, D_MODEL), F32)], axis=0)
    w_r = jnp.concatenate([cols(_FOX0, _FOX0 + 768), cols(_DIF0, _DIF0 + 768), cols(_SB0, _SB0 + 768),
                           cols(_GDN0, _GDN0 + 768), cols(_GDN0 + 776, _GDN0 + 1032), small], axis=0)
    sp = jnp.zeros((8, LANES), F32)
    sp = sp.at[0, 0:4].set(p["b_fox_f"][l]).at[0, 4:8].set(p["gdn_dt_bias"][l]).at[1, 4:8].set(p["gdn_a_log"][l])
    return dict(
        w_r=_transpose_cast(w_r), sp=sp, w_gate=_transpose_cast(cols(_GATE0, _GATE0 + N_BRANCH * D_MODEL)),
        g1=p["norm1_g"][l].reshape(1, D_MODEL), g2=p["norm2_g"][l].reshape(1, D_MODEL),
        dl=jnp.stack([p["diff_lq1"][l], p["diff_lk1"][l], p["diff_lq2"][l], p["diff_lk2"][l]]),
        subln=jnp.tile(p["diff_subln_g"][l], 2).reshape(1, LANES),
        conv_w=p["gdn_conv_w"][l], norm_g=jnp.tile(p["gdn_norm_g"][l], N_HEADS).reshape(1, N_HEADS * HEAD_DIM),
        w_branch=p["w_branch"][l].astype(BF16), w_out=p["w_out"][l].astype(BF16),
        wg=p["w_ffn_gate"][l].astype(BF16), wu=p["w_ffn_up"][l].astype(BF16), wd=p["w_ffn_down"][l].astype(BF16),
    )


def _with_past(past, new16, t_pad):
    B, T, W = new16.shape
    parts = [new16] if past is None else [past.reshape(B, -1, W).astype(BF16), new16]
    n = sum(a.shape[1] for a in parts)
    if t_pad > n:
        parts.append(jnp.zeros((B, t_pad - n, W), BF16))
    return jnp.concatenate(parts, axis=1) if len(parts) > 1 else parts[0]


def _trunk(x, caches, p, layers, *, tm, tq, tk, sw, tb):
    B, T, _ = x.shape
    M = B * T
    past = 0 if caches is None else caches[0].shape[2]
    t_keys = -(-(past + T) // tk) * tk
    x2 = x.reshape(M, D_MODEL)
    new_state = []
    for l in range(DEPTH):
        lp = layers[l]
        c = None if caches is None else tuple(a[l] for a in caches)
        (fq, fk32, fk16, fv32, fv16, dq, dk32, dk16, dv32, dv16, sq, sk32, sk16, sv32, sv16,
         gqkv, gz, small) = _in_proj(x2, lp["g1"], lp["w_r"], lp["sp"], tm)
        b3 = lambda a: a.reshape(B, T, a.shape[-1])
        kv = lambda i, a: _with_past(None if c is None else c[i], b3(a), t_keys)

        logf = b3(small)[:, :, 0:N_HEADS]
        lf_all = logf if c is None else jnp.concatenate([c[2], logf], axis=1)
        r_pad = -(-t_keys // (16 * LANES)) * 16
        lf_all = jnp.pad(lf_all, ((0, 0), (0, r_pad * LANES - lf_all.shape[1]), (0, 0)))
        cum3 = _cumsum_time(jnp.swapaxes(lf_all, 1, 2).reshape(B, N_HEADS, r_pad, LANES))
        fk_c, key_norm2 = _head_pack(kv(0, fk16), cum3, "k")
        bounds = None
        if T > tq:
            assert tk == 512 and tk % LANES == 0
            nck = t_keys // tk
            norm = lax.cummax(jnp.sqrt(key_norm2[:, :, 0, 0:N_HEADS] * 1.03), axis=1)
            c_end = sum(a[:, :, tk // LANES - 1::tk // LANES, LANES - 1] for a in cum3)[:, :, :nck]
            pad = lambda a, fill: jnp.pad(a, ((0, 0), (0, 0), (0, LANES - nck)), constant_values=fill)
            rows = jnp.stack([pad(jnp.swapaxes(norm, 1, 2), 0.0), pad(c_end, -NEG_BIG)], axis=2)
            bounds = jnp.pad(rows, ((0, 0), (0, 0), (0, 6), (0, 0)))
        o_fox = _fox_attention(_head_pack(b3(fq), cum3, "q", past), fk_c, _head_pack(kv(1, fv16), None, "v"),
                               bounds, tq=tq, tk=tk, past=past)

        lam_init = 0.8 - 0.6 * math.exp(-0.3 * l)
        o_diff = _diff_attention(lp["dl"], lp["subln"], b3(dq), kv(3, dk16), _head_pack(kv(4, dv16), None, "v"),
                                 tq=tq, tk=tk, past=past, lam_init=lam_init)

        o_sb = _sb_attention(b3(sq), kv(5, sk16), kv(6, sv16), tq=tq, tk=tk, sw=sw, past=past)

        buf = jnp.zeros((B, CONV_WIDTH - 1, GDN_CONV_CH), F32) if c is None else c[8]
        s0 = jnp.zeros((B, N_HEADS, 64, 64), F32) if c is None else c[7]
        o_gdn, s_new, conv_state = _gdn(b3(gqkv), b3(gz), b3(small), lp["conv_w"], buf, s0, lp["norm_g"], tb)

        x1, h = _merge(x2, lp["g1"], lp["g2"], o_fox.reshape(M, 256), o_diff.reshape(M, 256),
                       o_sb.reshape(M, 256), o_gdn.reshape(M, 256), lp["w_gate"], lp["w_branch"], lp["w_out"], tm)
        x2 = _ffn(x1, h, lp["wg"], lp["wu"], lp["wd"], p["final_norm_g"].reshape(1, D_MODEL),
                  l == DEPTH - 1, tm)
        h4 = lambda a: a.reshape(B, T, N_HEADS, HEAD_DIM)
        new_state.append((h4(fk32), h4(fv32), logf, h4(dk32), h4(dv32), h4(sk32), h4(sv32), s_new, conv_state))
    y = x2.reshape(B, T, D_MODEL)
    return y, tuple(jnp.stack(parts) for parts in zip(*new_state))


def kernel(x_prompt, x_sample, cache_fox_k, cache_fox_v, cache_fox_logf, cache_diff_k, cache_diff_v, cache_sb_k, cache_sb_v, state_gdn, state_gdn_conv, norm1_g, w_in, b_fox_f, diff_lq1, diff_lk1, diff_lq2, diff_lk2, diff_subln_g, gdn_conv_w, gdn_a_log, gdn_dt_bias, gdn_norm_g, w_branch, w_out, norm2_g, w_ffn_gate, w_ffn_up, w_ffn_down, final_norm_g):
    p = dict(norm1_g=norm1_g, w_in=w_in, b_fox_f=b_fox_f, diff_lq1=diff_lq1, diff_lk1=diff_lk1, diff_lq2=diff_lq2,
             diff_lk2=diff_lk2, diff_subln_g=diff_subln_g, gdn_conv_w=gdn_conv_w, gdn_a_log=gdn_a_log,
             gdn_dt_bias=gdn_dt_bias, gdn_norm_g=gdn_norm_g, w_branch=w_branch, w_out=w_out, norm2_g=norm2_g,
             w_ffn_gate=w_ffn_gate, w_ffn_up=w_ffn_up, w_ffn_down=w_ffn_down, final_norm_g=final_norm_g)
    layers = [_prep_layer(p, l) for l in range(DEPTH)]
    t_p = x_prompt.shape[1]
    blk = min(512, t_p)
    y_prompt, sp = _trunk(x_prompt, None, p, layers, tm=blk, tq=blk, tk=blk, sw=min(256, blk), tb=blk)
    caches = (cache_fox_k, cache_fox_v, cache_fox_logf, cache_diff_k, cache_diff_v, cache_sb_k, cache_sb_v,
              state_gdn, state_gdn_conv)
    b_s, t_s = x_sample.shape[0], x_sample.shape[1]
    keys_s = -(-(cache_fox_k.shape[2] + t_s) // LANES) * LANES
    y_sample, ss = _trunk(x_sample, caches, p, layers, tm=min(512, b_s * t_s), tq=t_s, tk=keys_s, sw=LANES, tb=t_s)
    return (y_prompt, y_sample) + sp + ss
```

```python
import functools
import math

import jax
import jax.numpy as jnp
from jax import lax
from jax.experimental import pallas as pl
from jax.experimental.pallas import tpu as pltpu

F32 = jnp.float32
BF16 = jnp.bfloat16

D_MODEL = 1024
DEPTH = 2
CHUNK = 64
HEAD_DIM = 64
N_HEADS = 4
DIFF_DK = 32
CONV_WIDTH = 4
BRANCH_W = 256
N_BRANCH = 4
D_FF = 2816
EPS = 1e-6
GDN_CONV_CH = 768

LANES = 128
NEG_BIG = -1e30
VMEM_LIMIT = 56 * 1024 * 1024
LOG2E = 1.4426950408889634
SB_CUTOFF = 105.0
FOX_CUTOFF = -160.0

_FOX0, _DIF0, _SB0, _GDN0, _GATE0 = 0, 772, 1540, 2308, 3340
N_PROJ = 3456

_NN = (((1,), (0,)), ((), ()))
_NT = (((1,), (1,)), ((), ()))
_TN = (((0,), (0,)), ((), ()))


def _dot(a, b):
    return jnp.dot(a, b, preferred_element_type=F32)


def _dot_nt(a, b):
    return lax.dot_general(a, b, _NT, preferred_element_type=F32)


def _split2(a):
    hi = a.astype(BF16)
    return hi, (a - hi.astype(F32)).astype(BF16)


def _dot_sel(x, sel, x_first=True):
    hi = x.astype(BF16)
    r = x - hi.astype(F32)
    mid = r.astype(BF16)
    lo = (r - mid.astype(F32)).astype(BF16)
    if x_first:
        return _dot(hi, sel) + (_dot(mid, sel) + _dot(lo, sel))
    return _dot(sel, hi) + (_dot(sel, mid) + _dot(sel, lo))


def _rms(x, g):
    return x * lax.rsqrt(jnp.mean(x * x, axis=-1, keepdims=True) + EPS) * g


def _softplus(t):
    return jnp.maximum(t, 0.0) + jnp.log1p(jnp.exp(-jnp.abs(t)))


def _sigmoid(t):
    return 1.0 / (1.0 + jnp.exp(-t))


def _const_spec(shape):
    nd = len(shape)
    return pl.BlockSpec(shape, lambda *_: (0,) * nd, pipeline_mode=pl.Buffered(1))


def _params(sem):
    return pltpu.CompilerParams(dimension_semantics=sem, vmem_limit_bytes=VMEM_LIMIT)


def _in_proj_kernel(x_ref, g_ref, w_ref, sp_ref,
                    fq_ref, fk32_ref, fk16_ref, fv32_ref, fv16_ref,
                    dq_ref, dk32_ref, dk16_ref, dv32_ref, dv16_ref,
                    sq_ref, sk32_ref, sk16_ref, sv32_ref, sv16_ref,
                    gqkv_ref, gz_ref, small_ref):
    u = _rms(x_ref[...], g_ref[...]).astype(BF16)

    def mm(c0, w):
        return _dot(u, w_ref[:, c0:c0 + w])

    def qkv(c0, q_scale, q_ref, k32, k16, v32, v16):
        q_ref[...] = (mm(c0, 256) * q_scale).astype(BF16)
        k = mm(c0 + 256, 256)
        k32[...] = k
        k16[...] = k.astype(BF16)
        v = mm(c0 + 512, 256)
        v32[...] = v
        v16[...] = v.astype(BF16)

    qkv(0, HEAD_DIM ** -0.5 * LOG2E, fq_ref, fk32_ref, fk16_ref, fv32_ref, fv16_ref)
    qkv(768, DIFF_DK ** -0.5 * LOG2E, dq_ref, dk32_ref, dk16_ref, dv32_ref, dv16_ref)
    qkv(1536, HEAD_DIM ** -0.5, sq_ref, sk32_ref, sk16_ref, sv32_ref, sv16_ref)
    gqkv_ref[...] = mm(2304, 768)
    gz_ref[...] = mm(3072, 256)
    t = mm(3328, 128) + sp_ref[0:1, :]
    lane = lax.broadcasted_iota(jnp.int32, (1, LANES), 1)
    sp = _softplus(t)
    logf = t - sp
    g = -jnp.exp(sp_ref[1:2, :]) * sp
    beta = _sigmoid(t)
    small_ref[...] = jnp.where(lane < 4, logf, jnp.where(lane < 8, g, jnp.where(lane < 12, beta, 0.0)))


def _in_proj(x2, g1, w_r, sp, tm):
    M = x2.shape[0]
    row = lambda w: pl.BlockSpec((tm, w), lambda i: (i, 0))
    f32o = lambda w: jax.ShapeDtypeStruct((M, w), F32)
    b16o = lambda w: jax.ShapeDtypeStruct((M, w), BF16)
    qkv_shapes = [b16o(256), f32o(256), b16o(256), f32o(256), b16o(256)]
    out_shape = qkv_shapes * 3 + [f32o(768), f32o(256), f32o(128)]
    out_specs = [row(256)] * 15 + [row(768), row(256), row(128)]
    return pl.pallas_call(
        _in_proj_kernel,
        grid=(M // tm,),
        in_specs=[row(D_MODEL), _const_spec((1, D_MODEL)), _const_spec((D_MODEL, N_PROJ)), _const_spec((8, LANES))],
        out_specs=out_specs,
        out_shape=out_shape,
        compiler_params=_params(("parallel",)),
        name="in_proj",
    )(x2, g1, w_r, sp)


def _cumsum_kernel(x_ref, hi_ref, mid_ref, lo_ref):
    R = x_ref.shape[2]
    ii = lax.broadcasted_iota(jnp.int32, (LANES, LANES), 0)
    jj = lax.broadcasted_iota(jnp.int32, (LANES, LANES), 1)
    upper = (ii <= jj).astype(BF16)
    ri = lax.broadcasted_iota(jnp.int32, (R, R), 0)
    rj = lax.broadcasted_iota(jnp.int32, (R, R), 1)
    lower = (rj < ri).astype(BF16)
    for h in range(N_HEADS):
        within = _dot_sel(x_ref[0, h], upper)
        tot = jnp.broadcast_to(within[:, LANES - 1:LANES], (R, LANES))
        c = (within + _dot_sel(tot, lower, x_first=False)) * LOG2E
        hi = c.astype(BF16).astype(F32)
        r = c - hi
        mid = r.astype(BF16).astype(F32)
        hi_ref[0, h] = hi
        mid_ref[0, h] = mid
        lo_ref[0, h] = (r - mid).astype(BF16).astype(F32)


def _cumsum_time(x4):
    B, H, R, _ = x4.shape
    spec = pl.BlockSpec((1, H, R, LANES), lambda b: (b, 0, 0, 0))
    return pl.pallas_call(
        _cumsum_kernel, grid=(B,), in_specs=[spec], out_specs=[spec] * 3,
        out_shape=[jax.ShapeDtypeStruct(x4.shape, F32)] * 3,
        compiler_params=_params(("parallel",)), name="fox_cumsum",
    )(x4)


def _head_pack_kernel(*refs, kind, tb, sb, off0):
    x_ref = refs[0]
    o_ref = refs[4] if kind == "k" else refs[-1]
    src = lax.broadcasted_iota(jnp.int32, (2 * LANES, 4 * LANES), 0)
    dst = lax.broadcasted_iota(jnp.int32, (2 * LANES, 4 * LANES), 1)
    place = ((dst // LANES == src // HEAD_DIM) & (dst % LANES == src % HEAD_DIM)).astype(BF16)
    lane = lax.broadcasted_iota(jnp.int32, (1, 4 * LANES), 1) % LANES
    if kind == "v":
        ones = lane >= HEAD_DIM
    else:
        c_lane, one_lane = (HEAD_DIM, HEAD_DIM + 3) if kind == "q" else (HEAD_DIM + 3, HEAD_DIM)
        ones = (lane >= one_lane) & (lane < one_lane + 3)
        pj = lax.broadcasted_iota(jnp.int32, (16, 4 * LANES), 0)
        pd = lax.broadcasted_iota(jnp.int32, (16, 4 * LANES), 1)
        hit = (pj < 12) & (pd == LANES * (pj % 4) + c_lane + pj // 4)
        spread = jnp.where(hit, 1.0 if kind == "q" else -1.0, 0.0).astype(BF16)
        ti = lax.broadcasted_iota(jnp.int32, (sb, sb), 0)
        tj = lax.broadcasted_iota(jnp.int32, (sb, sb), 1)
        eye = (ti == tj).astype(BF16)
    if kind == "k":
        hs_i = lax.broadcasted_iota(jnp.int32, (2 * LANES, LANES), 0) // HEAD_DIM
        hs_j = lax.broadcasted_iota(jnp.int32, (2 * LANES, LANES), 1)
        head_sum = (hs_i == hs_j).astype(BF16)
        norm2 = jnp.zeros((1, LANES), F32)
    for s in range(tb // sb):
        out = _dot(x_ref[0, s * sb:(s + 1) * sb, :], place)
        if kind == "k":
            xf = x_ref[0, s * sb:(s + 1) * sb, :].astype(F32)
            norm2 = jnp.maximum(norm2, jnp.max(_dot((xf * xf).astype(BF16), head_sum), axis=0, keepdims=True))
        if kind != "v":
            off = off0 + s * sb
            row = off // LANES
            if tb % LANES == 0:
                row = row + pl.program_id(1) * (tb // LANES)
            lanes = slice(off % LANES, off % LANES + sb)
            pieces = [c_ref[0, h, pl.ds(row, 1), lanes] for c_ref in refs[1:4] for h in range(N_HEADS)]
            pieces = jnp.concatenate(pieces + [jnp.zeros((4, sb), F32)], axis=0).astype(BF16)
            columns = _dot_nt(eye, pieces).astype(BF16)
            out = out + _dot(columns, spread)
        o_ref[0, s * sb:(s + 1) * sb, :] = jnp.where(ones, 1.0, out).astype(o_ref.dtype)
    if kind == "k":
        refs[5][0, 0] = jnp.broadcast_to(norm2, (8, LANES))


def _head_pack(x16, c3, kind, off0=0):
    B, T, _ = x16.shape
    tb = 512 if T % 512 == 0 else T
    sb = min(LANES, tb)
    assert tb % sb == 0 and (tb == T or (tb % LANES == 0 and off0 % LANES == 0))
    assert kind == "v" or off0 % LANES + sb <= LANES
    specs = [pl.BlockSpec((1, tb, 2 * LANES), lambda b, t: (b, t, 0))]
    args = [x16]
    if kind != "v":
        R = c3[0].shape[2]
        specs += [pl.BlockSpec((1, N_HEADS, R, LANES), lambda b, t: (b, 0, 0, 0))] * 3
        args += list(c3)
    out_specs = [pl.BlockSpec((1, tb, 4 * LANES), lambda b, t: (b, t, 0))]
    out_shape = [jax.ShapeDtypeStruct((B, T, 4 * LANES), BF16)]
    if kind == "k":
        out_specs.append(pl.BlockSpec((1, 1, 8, LANES), lambda b, t: (b, t, 0, 0)))
        out_shape.append(jax.ShapeDtypeStruct((B, T // tb, 8, LANES), F32))
    out = pl.pallas_call(
        functools.partial(_head_pack_kernel, kind=kind, tb=tb, sb=sb, off0=off0),
        grid=(B, T // tb),
        in_specs=specs,
        out_specs=out_specs,
        out_shape=out_shape,
        compiler_params=_params(("parallel", "parallel")),
        name="head_pack_" + kind,
    )(*args)
    return out if kind == "k" else out[0]


def _lane_group(q, lo, hi):
    lane = lax.broadcasted_iota(jnp.int32, (1, LANES), 1)
    return jnp.where((lane >= lo) & (lane < hi), q, jnp.zeros_like(q))


def _chunk_plan(qi, *, tq, tk, past, single):
    q0 = past if single else past + qi * tq
    return q0 // tk, q0


def _for_chunks(n_full, n_diag, step):
    if isinstance(n_full, int):
        for j in range(n_full):
            step(j, False)
    else:
        def body(j, c):
            step(j, False)
            return c
        lax.fori_loop(0, n_full, body, 0)
    for r in range(n_diag):
        step(n_full + r, True)


def _rows(j, tk):
    return pl.ds(j * tk, tk) if isinstance(j, int) else pl.ds(pl.multiple_of(j * tk, tk), tk)


def _lane_tile(x, reps):
    return x if reps == 1 else jnp.concatenate([x] * reps, axis=1)


def _softmax_init(m_ref, acc_ref):
    m_ref[...] = jnp.full(m_ref.shape, NEG_BIG, F32)
    acc_ref[...] = jnp.zeros(acc_ref.shape, F32)


def _in_turns(chains):
    pending = list(chains)
    while pending:
        pending = [g for g in pending if next(g, "done") is None]


def _softmax_chain(logits, v1, m_ref, acc_ref, c):
    s = logits()
    yield
    m_old = m_ref[c]
    m_new = jnp.maximum(m_old, jnp.max(s, axis=1, keepdims=True))
    p = jnp.exp2(s - _lane_tile(m_new, s.shape[1] // LANES)).astype(BF16)
    yield
    acc_ref[c] = jnp.exp2(m_old - m_new) * acc_ref[c] + _dot(p, v1())
    m_ref[c] = m_new


def _normalised(acc):
    return acc / pltpu.roll(acc, HEAD_DIM, axis=1)


def _pair_lanes(even, odd):
    lane = lax.broadcasted_iota(jnp.int32, (1, LANES), 1)
    return jnp.where(lane < HEAD_DIM, even, pltpu.roll(odd, HEAD_DIM, axis=1))


def _fox_kernel(q_ref, k_ref, v_ref, *rest, tq, tk, past, single):
    bound_ref = None if single else rest[0]
    o_ref, m_ref, acc_ref = rest[-3:]
    n_full, q0 = _chunk_plan(pl.program_id(1), tq=tq, tk=tk, past=past, single=single)
    q_pos = q0 + lax.broadcasted_iota(jnp.int32, (tq, 1), 0)
    _softmax_init(m_ref, acc_ref)

    def step(j, masked):
        rows = _rows(j, tk)
        if masked:
            visible = j * tk + lax.broadcasted_iota(jnp.int32, (1, tk), 1) <= q_pos

        def logits(h):
            head = slice(LANES * h, LANES * (h + 1))
            s = _dot_nt(q_ref[0, :, head], k_ref[0, rows, head])
            return jnp.where(visible, s, NEG_BIG) if masked else s

        _in_turns(_softmax_chain(functools.partial(logits, h),
                                 lambda h=h: v_ref[0, rows, LANES * h:LANES * (h + 1)], m_ref, acc_ref, h)
                  for h in range(N_HEADS))

    n_diag = tq // tk if tq > tk else 1
    if single:
        _for_chunks(n_full, n_diag, step)
    else:
        for r in range(n_diag):
            step(n_full + r, True)
        lane = lax.broadcasted_iota(jnp.int32, (1, LANES), 1)
        need = None
        for h in range(N_HEADS):
            qh = q_ref[0, :, LANES * h:LANES * (h + 1)].astype(F32)
            q_norm = jnp.sqrt(jnp.max(jnp.sum(jnp.where(lane < HEAD_DIM, qh * qh, 0.0), axis=1, keepdims=True),
                                      axis=0, keepdims=True))
            c_q = jnp.sum(jnp.where((lane >= HEAD_DIM) & (lane < HEAD_DIM + 3), qh, 0.0), axis=1, keepdims=True)
            slack = jnp.max(c_q - m_ref[h][:, 0:1], axis=0, keepdims=True)
            kb = bound_ref[0, h]
            need_h = q_norm * kb[0:1, :] + slack - kb[1:2, :] >= FOX_CUTOFF
            need = need_h if need is None else need | need_h
        first = jnp.min(jnp.where(need & (lane < n_full), lane, n_full))

        def body(j, c):
            step(j, False)
            return c
        lax.fori_loop(first, n_full, body, 0)
    for pr in range(2):
        o_ref[0, :, LANES * pr:LANES * (pr + 1)] = _pair_lanes(
            _normalised(acc_ref[2 * pr]), _normalised(acc_ref[2 * pr + 1])).astype(o_ref.dtype)


def _diff_kernel(dl_ref, g_ref, q_ref, k_ref, v_ref, o_ref, m_ref, acc_ref, *, tq, tk, past, single, lam_init):
    n_full, q0 = _chunk_plan(pl.program_id(2), tq=tq, tk=tk, past=past, single=single)
    q_chunk = (q0 + lax.broadcasted_iota(jnp.int32, (tq, 1), 0)) // CHUNK
    q = q_ref[0]
    qg = [_lane_group(q, DIFF_DK * g, DIFF_DK * (g + 1)) for g in range(4)]
    _softmax_init(m_ref, acc_ref)

    def step(j, masked):
        rows = _rows(j, tk)
        k = k_ref[0, rows, :]
        if masked:
            visible = (j * tk + lax.broadcasted_iota(jnp.int32, (1, tk), 1)) // CHUNK <= q_chunk

        def logits(g):
            s = _dot_nt(qg[g], k)
            return jnp.where(visible, s, NEG_BIG) if masked else s

        _in_turns(_softmax_chain(functools.partial(logits, g),
                                 lambda g=g: v_ref[0, rows, LANES * (g // 2):LANES * (g // 2 + 1)], m_ref, acc_ref, g)
                  for g in range(4))

    _for_chunks(n_full, tq // tk if tq > tk else 1, step)
    dl = dl_ref[...]
    lam = (jnp.exp(jnp.sum(dl[0:1] * dl[1:2], axis=1, keepdims=True))
           - jnp.exp(jnp.sum(dl[2:3] * dl[3:4], axis=1, keepdims=True)) + lam_init)
    p = [_normalised(acc_ref[g]) for g in range(4)]
    lane = lax.broadcasted_iota(jnp.int32, (1, LANES), 1)
    first = lane < 64
    o = _pair_lanes(p[0] - lam * p[1], p[2] - lam * p[3])
    sq = o * o
    ms = jnp.where(first,
                   jnp.sum(jnp.where(first, sq, 0.0), axis=1, keepdims=True),
                   jnp.sum(jnp.where(first, 0.0, sq), axis=1, keepdims=True)) * (1.0 / HEAD_DIM)
    o_ref[0] = (o * lax.rsqrt(ms + EPS) * g_ref[...] * (1.0 - lam_init)).astype(o_ref.dtype)


def _sb_kernel(q_ref, k_ref, v_ref, o_ref, right_ref, acc_ref, *, tq, tk, sw, past, single):
    n_full, q0 = _chunk_plan(pl.program_id(1), tq=tq, tk=tk, past=past, single=single)
    q_pos = q0 + lax.broadcasted_iota(jnp.int32, (tq, 1), 0)
    ki = lax.broadcasted_iota(jnp.int32, (2 * sw, sw), 0) % sw
    kj = lax.broadcasted_iota(jnp.int32, (2 * sw, sw), 1)
    after = (ki > kj).astype(BF16)
    right_ref[...] = jnp.zeros(right_ref.shape, F32)
    acc_ref[...] = jnp.zeros(acc_ref.shape, F32)
    n_diag = tq // tk if tq > tk else 1

    def step(j, masked):
        def chain(sub, h):
            use_mask = masked and not (single and j * tk + (sub + 1) * sw <= past)
            rows = (pl.ds(j * tk + sub * sw, sw) if isinstance(j, int)
                    else pl.ds(pl.multiple_of(j * tk + sub * sw, sw), sw))
            if use_mask:
                mask = j * tk + sub * sw + lax.broadcasted_iota(jnp.int32, (1, sw), 1) < q_pos
            pr = slice(LANES * (h // 2), LANES * (h // 2 + 1))
            qh = _lane_group(q_ref[0, :, pr], HEAD_DIM * (h % 2), HEAD_DIM * (h % 2 + 1))
            z = _dot_nt(qh, k_ref[0, rows, pr])
            yield
            sp = jnp.maximum(z, 0.0) + jnp.log(1.0 + jnp.exp2(jnp.abs(z) * -LOG2E))
            if use_mask:
                sp = jnp.where(mask, sp, 0.0)
            hi, lo = _split2(sp)
            yield
            later = _dot(jnp.concatenate([hi, lo], axis=1), after)
            yield
            right = right_ref[h]
            a = jnp.exp(z - sp - later - _lane_tile(right, sw // LANES))
            if use_mask:
                a = jnp.where(mask, a, 0.0)
            right_ref[h] = right + jnp.sum(sp, axis=1, keepdims=True)
            yield
            acc_ref[h] += _dot(a.astype(BF16), v_ref[0, rows, pr])

        _in_turns(chain(sub, h) for sub in reversed(range(tk // sw)) for h in range(N_HEADS))

    for r in reversed(range(n_diag)):
        step(n_full + r, True)
    if isinstance(n_full, int):
        for j in reversed(range(n_full)):
            step(j, False)
    else:
        def more(c):
            return (c[0] < n_full) & c[1]

        def body(c):
            step(n_full - 1 - c[0], False)
            return c[0] + 1, jnp.min(right_ref[...]) < SB_CUTOFF
        lax.while_loop(more, body, (jnp.int32(0), jnp.min(right_ref[...]) < SB_CUTOFF))
    lane = lax.broadcasted_iota(jnp.int32, (1, LANES), 1)
    for pr in range(2):
        o_ref[0, :, LANES * pr:LANES * (pr + 1)] = jnp.where(
            lane < 64, acc_ref[2 * pr], acc_ref[2 * pr + 1]).astype(o_ref.dtype)


def _check_tiles(Tq, Tk, tq, tk, past):
    single = Tq == tq
    assert Tq % tq == 0 and tq % CHUNK == 0 and Tk % tk == 0 and past + Tq <= Tk
    if single:
        n_full = past // tk
        assert (n_full + max(tq // tk, 1)) * tk >= past + Tq
    else:
        assert tq % tk == 0 and past % tk == 0
    return single


def _fox_attention(q, k, v, bounds, *, tq, tk, past):
    B, Tq, _ = q.shape
    Tk = k.shape[1]
    single = _check_tiles(Tq, Tk, tq, tk, past)
    assert single or Tk // tk <= LANES
    keys = pl.BlockSpec((1, Tk, 4 * LANES), lambda b, i: (b, 0, 0), pipeline_mode=pl.Buffered(1))
    extra_specs = [] if single else [pl.BlockSpec((1, N_HEADS, 8, LANES), lambda b, i: (b, 0, 0, 0))]
    extra = [] if single else [bounds]
    return pl.pallas_call(
        functools.partial(_fox_kernel, tq=tq, tk=tk, past=past, single=single),
        grid=(B, Tq // tq),
        in_specs=[pl.BlockSpec((1, tq, 4 * LANES), lambda b, i: (b, i, 0)), keys, keys] + extra_specs,
        out_specs=pl.BlockSpec((1, tq, 2 * LANES), lambda b, i: (b, i, 0)),
        out_shape=jax.ShapeDtypeStruct((B, Tq, 2 * LANES), BF16),
        scratch_shapes=[pltpu.VMEM((N_HEADS, tq, LANES), F32), pltpu.VMEM((N_HEADS, tq, LANES), F32)],
        compiler_params=_params(("parallel", "arbitrary")),
        name="fox_attention",
    )(q, k, v, *extra)


def _diff_attention(dl, subln, q, k, v, *, tq, tk, past, lam_init):
    B, Tq, _ = q.shape
    Tk = k.shape[1]
    single = _check_tiles(Tq, Tk, tq, tk, past)
    blk = pl.BlockSpec((1, tq, LANES), lambda b, p, i: (b, i, p))
    keys = pl.BlockSpec((1, Tk, LANES), lambda b, p, i: (b, 0, p))
    values = pl.BlockSpec((1, Tk, 2 * LANES), lambda b, p, i: (b, 0, p))
    return pl.pallas_call(
        functools.partial(_diff_kernel, tq=tq, tk=tk, past=past, single=single, lam_init=lam_init),
        grid=(B, 2, Tq // tq),
        in_specs=[pl.BlockSpec((4, DIFF_DK), lambda b, p, i: (0, 0)),
                  pl.BlockSpec((1, LANES), lambda b, p, i: (0, 0)), blk, keys, values],
        out_specs=blk,
        out_shape=jax.ShapeDtypeStruct((B, Tq, 2 * LANES), BF16),
        scratch_shapes=[pltpu.VMEM((4, tq, LANES), F32), pltpu.VMEM((4, tq, LANES), F32)],
        compiler_params=_params(("parallel", "parallel", "arbitrary")),
        name="diff_attention",
    )(dl, subln, q, k, v)


def _sb_attention(q, k, v, *, tq, tk, sw, past):
    B, Tq, _ = q.shape
    Tk = k.shape[1]
    single = _check_tiles(Tq, Tk, tq, tk, past)
    assert tk % sw == 0
    kv = pl.BlockSpec((1, Tk, 2 * LANES), lambda b, i: (b, 0, 0))
    blk = pl.BlockSpec((1, tq, 2 * LANES), lambda b, i: (b, i, 0))
    return pl.pallas_call(
        functools.partial(_sb_kernel, tq=tq, tk=tk, sw=sw, past=past, single=single),
        grid=(B, Tq // tq),
        in_specs=[blk, kv, kv],
        out_specs=blk,
        out_shape=jax.ShapeDtypeStruct((B, Tq, 2 * LANES), BF16),
        scratch_shapes=[pltpu.VMEM((N_HEADS, tq, LANES), F32), pltpu.VMEM((N_HEADS, tq, LANES), F32)],
        compiler_params=_params(("parallel", "arbitrary")),
        name="sb_attention",
    )(q, k, v)


def _mm3(a2, b2, dims=_NN):
    (ah, al), (bh, bl) = a2, b2
    dg = lambda x, y: lax.dot_general(x, y, dims, preferred_element_type=F32)
    return dg(ah, bh) + (dg(ah, bl) + dg(al, bh))


def _head_blocks(x):
    lane_head = lax.broadcasted_iota(jnp.int32, (1, 256), 1) // HEAD_DIM
    return jnp.concatenate([jnp.where(lane_head == h, x, 0.0) for h in range(N_HEADS)], axis=0)


def _fold_blocks(x):
    C = CHUNK
    return (x[0:C] + x[C:2 * C]) + (x[2 * C:3 * C] + x[3 * C:4 * C])


def _gdn_kernel(x_ref, z_ref, sm_ref, cw_ref, buf_ref, s0_ref, ng_ref,
                o_ref, sout_ref, cout_ref,
                xs_ref, qkv_ref, u_ref, w_ref, qe_ref, ke_ref, qk_ref, gl_ref, st_ref, *, tb):
    t = pl.program_id(1)
    C = CHUNK
    W = N_HEADS * HEAD_DIM
    tail = CONV_WIDTH - 1
    n_chunks = tb // C

    ri = lax.broadcasted_iota(jnp.int32, (W, W), 0)
    rj = lax.broadcasted_iota(jnp.int32, (W, W), 1)
    same_head = ri // HEAD_DIM == rj // HEAD_DIM
    same_head_b = same_head.astype(BF16)
    incl_blk = same_head & (ri % C >= rj % C)
    strict_blk = same_head & (ri % C > rj % C)
    eye = (ri == rj).astype(F32)

    @pl.when(t == 0)
    def _():
        for h in range(N_HEADS):
            st_ref[C * h:C * (h + 1), :] = jnp.concatenate(
                [s0_ref[0, h] if g == h else jnp.zeros((C, HEAD_DIM), F32) for g in range(N_HEADS)], axis=1)
        xs_ref[0:8, :] = jnp.zeros((8, GDN_CONV_CH), F32)
        xs_ref[8 - tail:8, :] = buf_ref[0]

    @pl.when(t > 0)
    def _():
        xs_ref[0:8, :] = xs_ref[tb:tb + 8, :]

    xs_ref[8:, :] = x_ref[0]
    y = cw_ref[0:1, :] * xs_ref[8 - tail:8 - tail + tb, :]
    for i in range(1, CONV_WIDTH):
        y = y + cw_ref[i:i + 1, :] * xs_ref[8 - tail + i:8 - tail + i + tb, :]
    y = y * _sigmoid(y)
    q = y[:, 0:W]
    k = y[:, W:2 * W]
    qkv_ref[:, 0:W] = q * lax.rsqrt(_dot_sel(q * q, same_head_b) + EPS) * (HEAD_DIM ** -0.5)
    qkv_ref[:, W:2 * W] = k * lax.rsqrt(_dot_sel(k * k, same_head_b) + EPS)
    qkv_ref[:, 2 * W:3 * W] = y[:, 2 * W:3 * W]

    ci = lax.broadcasted_iota(jnp.int32, (C, C), 0)
    cj = lax.broadcasted_iota(jnp.int32, (C, C), 1)
    incl_b = (ci >= cj).astype(BF16)
    li = lax.broadcasted_iota(jnp.int32, (LANES, W), 0)
    lj = lax.broadcasted_iota(jnp.int32, (LANES, W), 1) // HEAD_DIM
    spread_g = (li == 4 + lj).astype(BF16)
    spread_beta = (li == 8 + lj).astype(BF16)
    token_is_lane = (lax.broadcasted_iota(jnp.int32, (C, W), 0)
                     == lax.broadcasted_iota(jnp.int32, (C, W), 1) % C).astype(F32)
    ones_rows = jnp.ones((W, C), BF16)

    def prepare(c):
        rows = pl.ds(c * C, C) if isinstance(c, int) else pl.ds(pl.multiple_of(c * C, C), C)
        sm = sm_ref[0, rows, :]
        gcum = _dot_sel(_dot_sel(sm, incl_b, x_first=False), spread_g)
        beta = _dot_sel(sm, spread_beta)
        qn = qkv_ref[rows, 0:W]
        kn = qkv_ref[rows, W:2 * W]
        vn = qkv_ref[rows, 2 * W:3 * W]
        g_of_col = _dot_sel(gcum * token_is_lane, ones_rows, x_first=False)
        g_of_row = jnp.concatenate([gcum] * N_HEADS, axis=0)
        decay = jnp.where(incl_blk, jnp.exp(jnp.where(incl_blk, g_of_row - g_of_col, 0.0)), 0.0)
        kb = kn * beta
        k16 = _head_blocks(kn).astype(BF16)
        m = jnp.where(strict_blk, _dot_nt(_head_blocks(kb).astype(BF16), k16) * decay, 0.0)
        pw16 = (-m).astype(BF16)
        x = eye - m
        yield
        for _ in range(5):
            pw16 = _dot(pw16, pw16).astype(BF16)
            yield
            x = x + _dot(x.astype(BF16), pw16)
            yield
        x2 = _split2(x)
        res = eye - _mm3(_split2(eye + m), x2)
        yield
        inv = x + _mm3(x2, _split2(res))
        yield
        inv2 = _split2(inv)
        eg = jnp.exp(gcum)
        g_last = gcum[C - 1:C, :]
        u_ref[rows, :] = _fold_blocks(_mm3(inv2, _split2(_head_blocks(vn * beta))))
        w_ref[rows, :] = _fold_blocks(_mm3(inv2, _split2(_head_blocks(kb * eg))))
        qk_ref[rows, :] = _fold_blocks(
            jnp.where(incl_blk, _dot_nt(_head_blocks(qn).astype(BF16), k16) * decay, 0.0))
        qe_ref[rows, :] = qn * eg
        ke_ref[rows, :] = kn * jnp.exp(g_last - gcum)
        gl_rows = pl.ds(c * 8, 8) if isinstance(c, int) else pl.ds(pl.multiple_of(c * 8, 8), 8)
        gl_ref[gl_rows, :] = jnp.broadcast_to(jnp.exp(g_last), (8, W))

    def prepare_together(chunks):
        pending = [prepare(c) for c in chunks]
        while pending:
            pending = [g for g in pending if next(g, "done") is None]

    group = 4 if n_chunks % 4 == 0 else 1
    if n_chunks == group or group == 1 and n_chunks < 4:
        prepare_together(range(n_chunks))
    else:
        def prep_group(i, carry):
            prepare_together([group * i + r for r in range(group)])
            return carry
        lax.fori_loop(0, n_chunks // group, prep_group, 0)

    def scan(c, S):
        rows = pl.ds(c * C, C) if isinstance(c, int) else pl.ds(pl.multiple_of(c * C, C), C)
        gl_rows = pl.ds(c * 8, 8) if isinstance(c, int) else pl.ds(pl.multiple_of(c * 8, 8), 8)
        S16 = S.astype(BF16)
        v_new = u_ref[rows, :] - _dot(w_ref[rows, :].astype(BF16), S16)
        v16 = v_new.astype(BF16)
        o = _dot(qe_ref[rows, :].astype(BF16), S16) + _dot(qk_ref[rows, :].astype(BF16), _head_blocks(v16))
        S = S * gl_ref[gl_rows, :][0:1, :] + jnp.where(
            same_head, lax.dot_general(ke_ref[rows, :].astype(BF16), v16, _TN, preferred_element_type=F32), 0.0)
        ms = _dot_sel(o * o, same_head_b) * (1.0 / HEAD_DIM)
        zn = z_ref[0, rows, :]
        o_ref[0, rows, :] = (o * lax.rsqrt(ms + EPS) * ng_ref[...] * (zn * _sigmoid(zn))).astype(o_ref.dtype)
        return S

    S = st_ref[...]
    if n_chunks == 1:
        S = scan(0, S)
    else:
        S = lax.fori_loop(0, n_chunks, scan, S)
    st_ref[...] = S

    @pl.when(t == pl.num_programs(1) - 1)
    def _():
        for h in range(N_HEADS):
            sout_ref[0, h] = S[C * h:C * (h + 1), HEAD_DIM * h:HEAD_DIM * (h + 1)]
        cout_ref[0] = xs_ref[8 + tb - tail:8 + tb, :]


def _gdn(x, z, small, conv_w, buf, s0, norm_g, tb):
    B, T, _ = x.shape
    assert T % tb == 0 and tb % CHUNK == 0 and tb >= 8
    W = N_HEADS * HEAD_DIM
    tok = lambda w: pl.BlockSpec((1, tb, w), lambda b, t: (b, t, 0))
    per_b = lambda s: pl.BlockSpec((1,) + s, lambda b, t: (b,) + (0,) * len(s))
    natural = pltpu.VMEM((tb, W), F32)
    return pl.pallas_call(
        functools.partial(_gdn_kernel, tb=tb),
        grid=(B, T // tb),
        in_specs=[tok(768), tok(256), tok(128),
                  pl.BlockSpec((CONV_WIDTH, 768), lambda b, t: (0, 0)),
                  per_b((CONV_WIDTH - 1, 768)), per_b((N_HEADS, 64, 64)),
                  pl.BlockSpec((1, W), lambda b, t: (0, 0))],
        out_specs=[tok(256), per_b((N_HEADS, 64, 64)), per_b((CONV_WIDTH - 1, 768))],
        out_shape=[jax.ShapeDtypeStruct((B, T, 256), BF16),
                   jax.ShapeDtypeStruct((B, N_HEADS, 64, 64), F32),
                   jax.ShapeDtypeStruct((B, CONV_WIDTH - 1, 768), F32)],
        scratch_shapes=[pltpu.VMEM((tb + 8, 768), F32), pltpu.VMEM((tb, 768), F32)] + [natural] * 5
                       + [pltpu.VMEM((tb // CHUNK * 8, W), F32), pltpu.VMEM((W, W), F32)],
        compiler_params=_params(("parallel", "arbitrary")),
        name="gated_deltanet",
    )(x, z, small, conv_w, buf, s0, norm_g)


def _merge_kernel(x_ref, g1_ref, g2_ref, of_ref, od_ref, os_ref, og_ref, wgate_ref, wb_ref, wo_ref,
                  x1_ref, h_ref):
    x = x_ref[...]
    u = _rms(x, g1_ref[...]).astype(BF16)
    merged = None
    for i, o_ref in enumerate((of_ref, od_ref, os_ref, og_ref)):
        gate = _sigmoid(_dot(u, wgate_ref[:, i * D_MODEL:(i + 1) * D_MODEL]))
        term = gate * _dot(o_ref[...], wb_ref[i])
        merged = term if merged is None else merged + term
    x1 = x + _dot(merged.astype(BF16), wo_ref[...])
    x1_ref[...] = x1
    h_ref[...] = _rms(x1, g2_ref[...]).astype(BF16)


def _merge(x2, g1, g2, o_fox, o_diff, o_sb, o_gdn, w_gate, w_branch, w_out, tm):
    M = x2.shape[0]
    row = lambda w: pl.BlockSpec((tm, w), lambda i: (i, 0))
    return pl.pallas_call(
        _merge_kernel,
        grid=(M // tm,),
        in_specs=[row(D_MODEL), _const_spec((1, D_MODEL)), _const_spec((1, D_MODEL)),
                  row(256), row(256), row(256), row(256),
                  _const_spec((D_MODEL, N_BRANCH * D_MODEL)), _const_spec((N_BRANCH, BRANCH_W, D_MODEL)),
                  _const_spec((D_MODEL, D_MODEL))],
        out_specs=[row(D_MODEL), row(D_MODEL)],
        out_shape=[jax.ShapeDtypeStruct((M, D_MODEL), F32), jax.ShapeDtypeStruct((M, D_MODEL), BF16)],
        compiler_params=_params(("parallel",)),
        name="branch_merge",
    )(x2, g1, g2, o_fox, o_diff, o_sb, o_gdn, w_gate, w_branch, w_out)


def _ffn_kernel(x1_ref, h_ref, wg_ref, wu_ref, wd_ref, gf_ref, o_ref, *, final):
    h = h_ref[...]
    acc = x1_ref[...]
    half = D_FF // 2
    for f0 in (0, half):
        a = _dot(h, wg_ref[:, f0:f0 + half])
        b = _dot(h, wu_ref[:, f0:f0 + half])
        acc = acc + _dot((a * _sigmoid(a) * b).astype(BF16), wd_ref[f0:f0 + half, :])
    if final:
        acc = _rms(acc, gf_ref[...])
    o_ref[...] = acc


def _ffn(x1, h, wg, wu, wd, g_final, final, tm):
    M = x1.shape[0]
    row = pl.BlockSpec((tm, D_MODEL), lambda i: (i, 0))
    return pl.pallas_call(
        functools.partial(_ffn_kernel, final=final),
        grid=(M // tm,),
        in_specs=[row, row, _const_spec((D_MODEL, D_FF)), _const_spec((D_MODEL, D_FF)),
                  _const_spec((D_FF, D_MODEL)), _const_spec((1, D_MODEL))],
        out_specs=row,
        out_shape=jax.ShapeDtypeStruct((M, D_MODEL), F32),
        compiler_params=_params(("parallel",)),
        name="swiglu",
    )(x1, h, wg, wu, wd, g_final)


def _transpose_cast_kernel(x_ref, o_ref):
    o_ref[...] = x_ref[...].T.astype(o_ref.dtype)


def _transpose_cast(wt):
    n, k = wt.shape
    assert n % LANES == 0
    return pl.pallas_call(
        _transpose_cast_kernel, grid=(n // LANES,),
        in_specs=[pl.BlockSpec((LANES, k), lambda i: (i, 0))],
        out_specs=pl.BlockSpec((k, LANES), lambda i: (0, i)),
        out_shape=jax.ShapeDtypeStruct((k, n), BF16),
        compiler_params=_params(("parallel",)), name="transpose_cast",
    )(wt)


def _prep_layer(p, l):
    wt = jnp.swapaxes(p["w_in"][l], 0, 1)
    cols = lambda a, b: wt[a:b]
    small = jnp.concatenate([cols(_FOX0 + 768, _FOX0 + 772), cols(_GDN0 + 768, _GDN0 + 776),
                             jnp.zeros((LANES - 12, D_MODEL), F32)], axis=0)
    w_r = jnp.concatenate([cols(_FOX0, _FOX0 + 768), cols(_DIF0, _DIF0 + 768), cols(_SB0, _SB0 + 768),
                           cols(_GDN0, _GDN0 + 768), cols(_GDN0 + 776, _GDN0 + 1032), small], axis=0)
    sp = jnp.zeros((8, LANES), F32)
    sp = sp.at[0, 0:4].set(p["b_fox_f"][l]).at[0, 4:8].set(p["gdn_dt_bias"][l]).at[1, 4:8].set(p["gdn_a_log"][l])
    return dict(
        w_r=_transpose_cast(w_r), sp=sp, w_gate=_transpose_cast(cols(_GATE0, _GATE0 + N_BRANCH * D_MODEL)),
        g1=p["norm1_g"][l].reshape(1, D_MODEL), g2=p["norm2_g"][l].reshape(1, D_MODEL),
        dl=jnp.stack([p["diff_lq1"][l], p["diff_lk1"][l], p["diff_lq2"][l], p["diff_lk2"][l]]),
        subln=jnp.tile(p["diff_subln_g"][l], 2).reshape(1, LANES),
        conv_w=p["gdn_conv_w"][l], norm_g=jnp.tile(p["gdn_norm_g"][l], N_HEADS).reshape(1, N_HEADS * HEAD_DIM),
        w_branch=p["w_branch"][l].astype(BF16), w_out=p["w_out"][l].astype(BF16),
        wg=p["w_ffn_gate"][l].astype(BF16), wu=p["w_ffn_up"][l].astype(BF16), wd=p["w_ffn_down"][l].astype(BF16),
    )


def _with_past(past, new16, t_pad):
    B, T, W = new16.shape
    parts = [new16] if past is None else [past.reshape(B, -1, W).astype(BF16), new16]
    n = sum(a.shape[1] for a in parts)
    if t_pad > n:
        parts.append(jnp.zeros((B, t_pad - n, W), BF16))
    return jnp.concatenate(parts, axis=1) if len(parts) > 1 else parts[0]


def _trunk(x, caches, p, layers, *, tm, tq, tk, sw, tb):
    B, T, _ = x.shape
    M = B * T
    past = 0 if caches is None else caches[0].shape[2]
    t_keys = -(-(past + T) // tk) * tk
    x2 = x.reshape(M, D_MODEL)
    new_state = []
    for l in range(DEPTH):
        lp = layers[l]
        c = None if caches is None else tuple(a[l] for a in caches)
        (fq, fk32, fk16, fv32, fv16, dq, dk32, dk16, dv32, dv16, sq, sk32, sk16, sv32, sv16,
         gqkv, gz, small) = _in_proj(x2, lp["g1"], lp["w_r"], lp["sp"], tm)
        b3 = lambda a: a.reshape(B, T, a.shape[-1])
        kv = lambda i, a: _with_past(None if c is None else c[i], b3(a), t_keys)

        logf = b3(small)[:, :, 0:N_HEADS]
        lf_all = logf if c is None else jnp.concatenate([c[2], logf], axis=1)
        r_pad = -(-t_keys // (16 * LANES)) * 16
        lf_all = jnp.pad(lf_all, ((0, 0), (0, r_pad * LANES - lf_all.shape[1]), (0, 0)))
        cum3 = _cumsum_time(jnp.swapaxes(lf_all, 1, 2).reshape(B, N_HEADS, r_pad, LANES))
        fk_c, key_norm2 = _head_pack(kv(0, fk16), cum3, "k")
        bounds = None
        if T > tq:
            assert tk == 512 and tk % LANES == 0
            nck = t_keys // tk
            norm = lax.cummax(jnp.sqrt(key_norm2[:, :, 0, 0:N_HEADS] * 1.03), axis=1)
            c_end = sum(a[:, :, tk // LANES - 1::tk // LANES, LANES - 1] for a in cum3)[:, :, :nck]
            pad = lambda a, fill: jnp.pad(a, ((0, 0), (0, 0), (0, LANES - nck)), constant_values=fill)
            rows = jnp.stack([pad(jnp.swapaxes(norm, 1, 2), 0.0), pad(c_end, -NEG_BIG)], axis=2)
            bounds = jnp.pad(rows, ((0, 0), (0, 0), (0, 6), (0, 0)))
        o_fox = _fox_attention(_head_pack(b3(fq), cum3, "q", past), fk_c, _head_pack(kv(1, fv16), None, "v"),
                               bounds, tq=tq, tk=tk, past=past)

        lam_init = 0.8 - 0.6 * math.exp(-0.3 * l)
        o_diff = _diff_attention(lp["dl"], lp["subln"], b3(dq), kv(3, dk16), _head_pack(kv(4, dv16), None, "v"),
                                 tq=tq, tk=tk, past=past, lam_init=lam_init)

        o_sb = _sb_attention(b3(sq), kv(5, sk16), kv(6, sv16), tq=tq, tk=tk, sw=sw, past=past)

        buf = jnp.zeros((B, CONV_WIDTH - 1, GDN_CONV_CH), F32) if c is None else c[8]
        s0 = jnp.zeros((B, N_HEADS, 64, 64), F32) if c is None else c[7]
        o_gdn, s_new, conv_state = _gdn(b3(gqkv), b3(gz), b3(small), lp["conv_w"], buf, s0, lp["norm_g"], tb)

        x1, h = _merge(x2, lp["g1"], lp["g2"], o_fox.reshape(M, 256), o_diff.reshape(M, 256),
                       o_sb.reshape(M, 256), o_gdn.reshape(M, 256), lp["w_gate"], lp["w_branch"], lp["w_out"], tm)
        x2 = _ffn(x1, h, lp["wg"], lp["wu"], lp["wd"], p["final_norm_g"].reshape(1, D_MODEL),
                  l == DEPTH - 1, tm)
        h4 = lambda a: a.reshape(B, T, N_HEADS, HEAD_DIM)
        new_state.append((h4(fk32), h4(fv32), logf, h4(dk32), h4(dv32), h4(sk32), h4(sv32), s_new, conv_state))
    y = x2.reshape(B, T, D_MODEL)
    return y, tuple(jnp.stack(parts) for parts in zip(*new_state))


def kernel(x_prompt, x_sample, cache_fox_k, cache_fox_v, cache_fox_logf, cache_diff_k, cache_diff_v, cache_sb_k, cache_sb_v, state_gdn, state_gdn_conv, norm1_g, w_in, b_fox_f, diff_lq1, diff_lk1, diff_lq2, diff_lk2, diff_subln_g, gdn_conv_w, gdn_a_log, gdn_dt_bias, gdn_norm_g, w_branch, w_out, norm2_g, w_ffn_gate, w_ffn_up, w_ffn_down, final_norm_g):
    p = dict(norm1_g=norm1_g, w_in=w_in, b_fox_f=b_fox_f, diff_lq1=diff_lq1, diff_lk1=diff_lk1, diff_lq2=diff_lq2,
             diff_lk2=diff_lk2, diff_subln_g=diff_subln_g, gdn_conv_w=gdn_conv_w, gdn_a_log=gdn_a_log,
             gdn_dt_bias=gdn_dt_bias, gdn_norm_g=gdn_norm_g, w_branch=w_branch, w_out=w_out, norm2_g=norm2_g,
             w_ffn_gate=w_ffn_gate, w_ffn_up=w_ffn_up, w_ffn_down=w_ffn_down, final_norm_g=final_norm_g)
    layers = [_prep_layer(p, l) for l in range(DEPTH)]
    t_p = x_prompt.shape[1]
    blk = min(512, t_p)
    y_prompt, sp = _trunk(x_prompt, None, p, layers, tm=blk, tq=blk, tk=blk, sw=min(256, blk), tb=blk)
    caches = (cache_fox_k, cache_fox_v, cache_fox_logf, cache_diff_k, cache_diff_v, cache_sb_k, cache_sb_v,
              state_gdn, state_gdn_conv)
    b_s, t_s = x_sample.shape[0], x_sample.shape[1]
    keys_s = -(-(cache_fox_k.shape[2] + t_s) // LANES) * LANES
    y_sample, ss = _trunk(x_sample, caches, p, layers, tm=min(512, b_s * t_s), tq=t_s, tk=keys_s, sw=LANES, tb=t_s)
    return (y_prompt, y_sample) + sp + ss
```

```python
import functools
import math

import jax
import jax.numpy as jnp
from jax import lax
from jax.experimental import pallas as pl
from jax.experimental.pallas import tpu as pltpu

F32 = jnp.float32
BF16 = jnp.bfloat16

D_MODEL = 1024
DEPTH = 2
CHUNK = 64
HEAD_DIM = 64
N_HEADS = 4
DIFF_DK = 32
CONV_WIDTH = 4
BRANCH_W = 256
N_BRANCH = 4
D_FF = 2816
EPS = 1e-6
GDN_CONV_CH = 768

LANES = 128
NEG_BIG = -1e30
VMEM_LIMIT = 56 * 1024 * 1024
LOG2E = 1.4426950408889634
SB_CUTOFF = 105.0
FOX_CUTOFF = -160.0

_FOX0, _DIF0, _SB0, _GDN0, _GATE0 = 0, 772, 1540, 2308, 3340
N_PROJ = 3456

_NN = (((1,), (0,)), ((), ()))
_NT = (((1,), (1,)), ((), ()))
_TN = (((0,), (0,)), ((), ()))


def _dot(a, b):
    return jnp.dot(a, b, preferred_element_type=F32)


def _dot_nt(a, b):
    return lax.dot_general(a, b, _NT, preferred_element_type=F32)


def _split2(a):
    hi = a.astype(BF16)
    return hi, (a - hi.astype(F32)).astype(BF16)


def _dot_sel(x, sel, x_first=True):
    hi = x.astype(BF16)
    r = x - hi.astype(F32)
    mid = r.astype(BF16)
    lo = (r - mid.astype(F32)).astype(BF16)
    if x_first:
        return _dot(hi, sel) + (_dot(mid, sel) + _dot(lo, sel))
    return _dot(sel, hi) + (_dot(sel, mid) + _dot(sel, lo))


def _rms(x, g):
    return x * lax.rsqrt(jnp.mean(x * x, axis=-1, keepdims=True) + EPS) * g


def _softplus(t):
    return jnp.maximum(t, 0.0) + jnp.log1p(jnp.exp(-jnp.abs(t)))


def _sigmoid(t):
    return 1.0 / (1.0 + jnp.exp(-t))


def _const_spec(shape):
    nd = len(shape)
    return pl.BlockSpec(shape, lambda *_: (0,) * nd, pipeline_mode=pl.Buffered(1))


def _params(sem):
    return pltpu.CompilerParams(dimension_semantics=sem, vmem_limit_bytes=VMEM_LIMIT)


def _in_proj_kernel(x_ref, g_ref, w_ref, sp_ref, *refs, time_minor):
    wkt_ref = refs[0] if time_minor else None
    (fq_ref, fk32_ref, fk16_ref, fv32_ref, fv16_ref, dq_ref, dk32_ref, dk16_ref, dv32_ref, dv16_ref,
     sq_ref, sk32_ref, sk16_ref, sv32_ref, sv16_ref, gqkv_ref, gz_ref, small_ref) = refs[1:] if time_minor else refs
    u = _rms(x_ref[...], g_ref[...]).astype(BF16)

    def mm(c0, w):
        return _dot(u, w_ref[:, c0:c0 + w])

    def cache_rows(ref, rows, n):
        if time_minor:
            ref[0] = _dot_nt(wkt_ref[256 * n:256 * (n + 1), :], u)
        else:
            ref[...] = rows

    def qkv(m, q_scale, q_ref, k32, k16, v32, v16):
        c0 = 768 * m
        q_ref[...] = (mm(c0, 256) * q_scale).astype(BF16)
        k = mm(c0 + 256, 256)
        cache_rows(k32, k, 2 * m)
        k16[...] = k.astype(BF16)
        v = mm(c0 + 512, 256)
        cache_rows(v32, v, 2 * m + 1)
        v16[...] = v.astype(BF16)

    qkv(0, HEAD_DIM ** -0.5 * LOG2E, fq_ref, fk32_ref, fk16_ref, fv32_ref, fv16_ref)
    qkv(1, DIFF_DK ** -0.5 * LOG2E, dq_ref, dk32_ref, dk16_ref, dv32_ref, dv16_ref)
    qkv(2, HEAD_DIM ** -0.5, sq_ref, sk32_ref, sk16_ref, sv32_ref, sv16_ref)
    gqkv_ref[...] = mm(2304, 768)
    gz_ref[...] = mm(3072, 256)
    t = mm(3328, 128) + sp_ref[0:1, :]
    lane = lax.broadcasted_iota(jnp.int32, (1, LANES), 1)
    sp = _softplus(t)
    logf = t - sp
    g = -jnp.exp(sp_ref[1:2, :]) * sp
    beta = _sigmoid(t)
    small_ref[...] = jnp.where(lane < 4, logf, jnp.where(lane < 8, g, jnp.where(lane < 12, beta, 0.0)))


def _in_proj(x2, g1, w_r, sp, tm, wkt=None, streams=1):
    M = x2.shape[0]
    T = M // streams
    time_minor = wkt is not None
    assert not time_minor or T % tm == 0
    row = lambda w: pl.BlockSpec((tm, w), lambda i: (i, 0))
    f32o = lambda w: jax.ShapeDtypeStruct((M, w), F32)
    b16o = lambda w: jax.ShapeDtypeStruct((M, w), BF16)
    cache_shape = jax.ShapeDtypeStruct((streams, 256, T), F32) if time_minor else f32o(256)
    cache_spec = (pl.BlockSpec((1, 256, tm), lambda i: (i // (T // tm), 0, i % (T // tm))) if time_minor
                  else row(256))
    out_shape = [b16o(256), cache_shape, b16o(256), cache_shape, b16o(256)] * 3 + [f32o(768), f32o(256), f32o(128)]
    out_specs = [row(256), cache_spec, row(256), cache_spec, row(256)] * 3 + [row(768), row(256), row(128)]
    in_specs = [row(D_MODEL), _const_spec((1, D_MODEL)), _const_spec((D_MODEL, N_PROJ)), _const_spec((8, LANES))]
    args = [x2, g1, w_r, sp]
    if time_minor:
        in_specs.append(_const_spec(wkt.shape))
        args.append(wkt)
    return pl.pallas_call(
        functools.partial(_in_proj_kernel, time_minor=time_minor),
        grid=(M // tm,),
        in_specs=in_specs,
        out_specs=out_specs,
        out_shape=out_shape,
        compiler_params=_params(("parallel",)),
        name="in_proj",
    )(*args)


def _cumsum_kernel(x_ref, hi_ref, mid_ref, lo_ref):
    R = x_ref.shape[2]
    ii = lax.broadcasted_iota(jnp.int32, (LANES, LANES), 0)
    jj = lax.broadcasted_iota(jnp.int32, (LANES, LANES), 1)
    upper = (ii <= jj).astype(BF16)
    ri = lax.broadcasted_iota(jnp.int32, (R, R), 0)
    rj = lax.broadcasted_iota(jnp.int32, (R, R), 1)
    lower = (rj < ri).astype(BF16)
    for h in range(N_HEADS):
        within = _dot_sel(x_ref[0, h], upper)
        tot = jnp.broadcast_to(within[:, LANES - 1:LANES], (R, LANES))
        c = (within + _dot_sel(tot, lower, x_first=False)) * LOG2E
        hi = c.astype(BF16).astype(F32)
        r = c - hi
        mid = r.astype(BF16).astype(F32)
        hi_ref[0, h] = hi
        mid_ref[0, h] = mid
        lo_ref[0, h] = (r - mid).astype(BF16).astype(F32)


def _cumsum_time(x4):
    B, H, R, _ = x4.shape
    spec = pl.BlockSpec((1, H, R, LANES), lambda b: (b, 0, 0, 0))
    return pl.pallas_call(
        _cumsum_kernel, grid=(B,), in_specs=[spec], out_specs=[spec] * 3,
        out_shape=[jax.ShapeDtypeStruct(x4.shape, F32)] * 3,
        compiler_params=_params(("parallel",)), name="fox_cumsum",
    )(x4)


def _head_pack_kernel(*refs, kind, tb, sb, off0):
    x_ref = refs[0]
    o_ref = refs[4] if kind == "k" else refs[-1]
    src = lax.broadcasted_iota(jnp.int32, (2 * LANES, 4 * LANES), 0)
    dst = lax.broadcasted_iota(jnp.int32, (2 * LANES, 4 * LANES), 1)
    place = ((dst // LANES == src // HEAD_DIM) & (dst % LANES == src % HEAD_DIM)).astype(BF16)
    lane = lax.broadcasted_iota(jnp.int32, (1, 4 * LANES), 1) % LANES
    if kind == "v":
        ones = lane >= HEAD_DIM
    else:
        c_lane, one_lane = (HEAD_DIM, HEAD_DIM + 3) if kind == "q" else (HEAD_DIM + 3, HEAD_DIM)
        ones = (lane >= one_lane) & (lane < one_lane + 3)
        pj = lax.broadcasted_iota(jnp.int32, (16, 4 * LANES), 0)
        pd = lax.broadcasted_iota(jnp.int32, (16, 4 * LANES), 1)
        hit = (pj < 12) & (pd == LANES * (pj % 4) + c_lane + pj // 4)
        spread = jnp.where(hit, 1.0 if kind == "q" else -1.0, 0.0).astype(BF16)
        ti = lax.broadcasted_iota(jnp.int32, (sb, sb), 0)
        tj = lax.broadcasted_iota(jnp.int32, (sb, sb), 1)
        eye = (ti == tj).astype(BF16)
    if kind == "k":
        hs_i = lax.broadcasted_iota(jnp.int32, (2 * LANES, LANES), 0) // HEAD_DIM
        hs_j = lax.broadcasted_iota(jnp.int32, (2 * LANES, LANES), 1)
        head_sum = (hs_i == hs_j).astype(BF16)
        norm2 = jnp.zeros((1, LANES), F32)
    for s in range(tb // sb):
        out = _dot(x_ref[0, s * sb:(s + 1) * sb, :], place)
        if kind == "k":
            xf = x_ref[0, s * sb:(s + 1) * sb, :].astype(F32)
            norm2 = jnp.maximum(norm2, jnp.max(_dot((xf * xf).astype(BF16), head_sum), axis=0, keepdims=True))
        if kind != "v":
            off = off0 + s * sb
            row = off // LANES
            if tb % LANES == 0:
                row = row + pl.program_id(1) * (tb // LANES)
            lanes = slice(off % LANES, off % LANES + sb)
            pieces = [c_ref[0, h, pl.ds(row, 1), lanes] for c_ref in refs[1:4] for h in range(N_HEADS)]
            pieces = jnp.concatenate(pieces + [jnp.zeros((4, sb), F32)], axis=0).astype(BF16)
            columns = _dot_nt(eye, pieces).astype(BF16)
            out = out + _dot(columns, spread)
        o_ref[0, s * sb:(s + 1) * sb, :] = jnp.where(ones, 1.0, out).astype(o_ref.dtype)
    if kind == "k":
        refs[5][0, 0] = jnp.broadcast_to(norm2, (8, LANES))


def _head_pack(x16, c3, kind, off0=0):
    B, T, _ = x16.shape
    tb = 512 if T % 512 == 0 else T
    sb = min(LANES, tb)
    assert tb % sb == 0 and (tb == T or (tb % LANES == 0 and off0 % LANES == 0))
    assert kind == "v" or off0 % LANES + sb <= LANES
    specs = [pl.BlockSpec((1, tb, 2 * LANES), lambda b, t: (b, t, 0))]
    args = [x16]
    if kind != "v":
        R = c3[0].shape[2]
        specs += [pl.BlockSpec((1, N_HEADS, R, LANES), lambda b, t: (b, 0, 0, 0))] * 3
        args += list(c3)
    out_specs = [pl.BlockSpec((1, tb, 4 * LANES), lambda b, t: (b, t, 0))]
    out_shape = [jax.ShapeDtypeStruct((B, T, 4 * LANES), BF16)]
    if kind == "k":
        out_specs.append(pl.BlockSpec((1, 1, 8, LANES), lambda b, t: (b, t, 0, 0)))
        out_shape.append(jax.ShapeDtypeStruct((B, T // tb, 8, LANES), F32))
    out = pl.pallas_call(
        functools.partial(_head_pack_kernel, kind=kind, tb=tb, sb=sb, off0=off0),
        grid=(B, T // tb),
        in_specs=specs,
        out_specs=out_specs,
        out_shape=out_shape,
        compiler_params=_params(("parallel", "parallel")),
        name="head_pack_" + kind,
    )(*args)
    return out if kind == "k" else out[0]


def _lane_group(q, lo, hi):
    lane = lax.broadcasted_iota(jnp.int32, (1, LANES), 1)
    return jnp.where((lane >= lo) & (lane < hi), q, jnp.zeros_like(q))


def _chunk_plan(qi, *, tq, tk, past, single):
    q0 = past if single else past + qi * tq
    return q0 // tk, q0


def _for_chunks(n_full, n_diag, step):
    if isinstance(n_full, int):
        for j in range(n_full):
            step(j, False)
    else:
        def body(j, c):
            step(j, False)
            return c
        lax.fori_loop(0, n_full, body, 0)
    for r in range(n_diag):
        step(n_full + r, True)


def _rows(j, tk):
    return pl.ds(j * tk, tk) if isinstance(j, int) else pl.ds(pl.multiple_of(j * tk, tk), tk)


def _lane_tile(x, reps):
    return x if reps == 1 else jnp.concatenate([x] * reps, axis=1)


def _softmax_init(m_ref, acc_ref):
    m_ref[...] = jnp.full(m_ref.shape, NEG_BIG, F32)
    acc_ref[...] = jnp.zeros(acc_ref.shape, F32)


def _in_turns(chains):
    pending = list(chains)
    while pending:
        pending = [g for g in pending if next(g, "done") is None]


def _softmax_chain(logits, v1, m_ref, acc_ref, c):
    s = logits()
    yield
    m_old = m_ref[c]
    m_new = jnp.maximum(m_old, jnp.max(s, axis=1, keepdims=True))
    p = jnp.exp2(s - _lane_tile(m_new, s.shape[1] // LANES)).astype(BF16)
    yield
    acc_ref[c] = jnp.exp2(m_old - m_new) * acc_ref[c] + _dot(p, v1())
    m_ref[c] = m_new


def _normalised(acc):
    return acc / pltpu.roll(acc, HEAD_DIM, axis=1)


def _pair_lanes(even, odd):
    lane = lax.broadcasted_iota(jnp.int32, (1, LANES), 1)
    return jnp.where(lane < HEAD_DIM, even, pltpu.roll(odd, HEAD_DIM, axis=1))


def _fox_kernel(q_ref, k_ref, v_ref, *rest, tq, tk, past, single):
    bound_ref = None if single else rest[0]
    o_ref, m_ref, acc_ref = rest[-3:]
    n_full, q0 = _chunk_plan(pl.program_id(1), tq=tq, tk=tk, past=past, single=single)
    q_pos = q0 + lax.broadcasted_iota(jnp.int32, (tq, 1), 0)
    _softmax_init(m_ref, acc_ref)

    def step(j, masked):
        rows = _rows(j, tk)
        if masked:
            visible = j * tk + lax.broadcasted_iota(jnp.int32, (1, tk), 1) <= q_pos

        def logits(h):
            head = slice(LANES * h, LANES * (h + 1))
            s = _dot_nt(q_ref[0, :, head], k_ref[0, rows, head])
            return jnp.where(visible, s, NEG_BIG) if masked else s

        _in_turns(_softmax_chain(functools.partial(logits, h),
                                 lambda h=h: v_ref[0, rows, LANES * h:LANES * (h + 1)], m_ref, acc_ref, h)
                  for h in range(N_HEADS))

    n_diag = tq // tk if tq > tk else 1
    if single:
        _for_chunks(n_full, n_diag, step)
    else:
        for r in range(n_diag):
            step(n_full + r, True)
        lane = lax.broadcasted_iota(jnp.int32, (1, LANES), 1)
        need = None
        for h in range(N_HEADS):
            qh = q_ref[0, :, LANES * h:LANES * (h + 1)].astype(F32)
            q_norm = jnp.sqrt(jnp.max(jnp.sum(jnp.where(lane < HEAD_DIM, qh * qh, 0.0), axis=1, keepdims=True),
                                      axis=0, keepdims=True))
            c_q = jnp.sum(jnp.where((lane >= HEAD_DIM) & (lane < HEAD_DIM + 3), qh, 0.0), axis=1, keepdims=True)
            slack = jnp.max(c_q - m_ref[h][:, 0:1], axis=0, keepdims=True)
            kb = bound_ref[0, h]
            need_h = q_norm * kb[0:1, :] + slack - kb[1:2, :] >= FOX_CUTOFF
            need = need_h if need is None else need | need_h
        first = jnp.min(jnp.where(need & (lane < n_full), lane, n_full))

        def body(j, c):
            step(j, False)
            return c
        lax.fori_loop(first, n_full, body, 0)
    for pr in range(2):
        o_ref[0, :, LANES * pr:LANES * (pr + 1)] = _pair_lanes(
            _normalised(acc_ref[2 * pr]), _normalised(acc_ref[2 * pr + 1])).astype(o_ref.dtype)


def _diff_kernel(dl_ref, g_ref, q_ref, k_ref, v_ref, o_ref, m_ref, acc_ref, *, tq, tk, past, single, lam_init):
    n_full, q0 = _chunk_plan(pl.program_id(2), tq=tq, tk=tk, past=past, single=single)
    q_chunk = (q0 + lax.broadcasted_iota(jnp.int32, (tq, 1), 0)) // CHUNK
    q = q_ref[0]
    qg = [_lane_group(q, DIFF_DK * g, DIFF_DK * (g + 1)) for g in range(4)]
    _softmax_init(m_ref, acc_ref)

    def step(j, masked):
        rows = _rows(j, tk)
        k = k_ref[0, rows, :]
        if masked:
            visible = (j * tk + lax.broadcasted_iota(jnp.int32, (1, tk), 1)) // CHUNK <= q_chunk

        def logits(g):
            s = _dot_nt(qg[g], k)
            return jnp.where(visible, s, NEG_BIG) if masked else s

        _in_turns(_softmax_chain(functools.partial(logits, g),
                                 lambda g=g: v_ref[0, rows, LANES * (g // 2):LANES * (g // 2 + 1)], m_ref, acc_ref, g)
                  for g in range(4))

    _for_chunks(n_full, tq // tk if tq > tk else 1, step)
    dl = dl_ref[...]
    lam = (jnp.exp(jnp.sum(dl[0:1] * dl[1:2], axis=1, keepdims=True))
           - jnp.exp(jnp.sum(dl[2:3] * dl[3:4], axis=1, keepdims=True)) + lam_init)
    p = [_normalised(acc_ref[g]) for g in range(4)]
    lane = lax.broadcasted_iota(jnp.int32, (1, LANES), 1)
    first = lane < 64
    o = _pair_lanes(p[0] - lam * p[1], p[2] - lam * p[3])
    sq = o * o
    ms = jnp.where(first,
                   jnp.sum(jnp.where(first, sq, 0.0), axis=1, keepdims=True),
                   jnp.sum(jnp.where(first, 0.0, sq), axis=1, keepdims=True)) * (1.0 / HEAD_DIM)
    o_ref[0] = (o * lax.rsqrt(ms + EPS) * g_ref[...] * (1.0 - lam_init)).astype(o_ref.dtype)


def _sb_kernel(q_ref, k_ref, v_ref, o_ref, right_ref, acc_ref, *, tq, tk, sw, past, single):
    n_full, q0 = _chunk_plan(pl.program_id(1), tq=tq, tk=tk, past=past, single=single)
    q_pos = q0 + lax.broadcasted_iota(jnp.int32, (tq, 1), 0)
    ki = lax.broadcasted_iota(jnp.int32, (2 * sw, sw), 0) % sw
    kj = lax.broadcasted_iota(jnp.int32, (2 * sw, sw), 1)
    after = (ki > kj).astype(BF16)
    right_ref[...] = jnp.zeros(right_ref.shape, F32)
    acc_ref[...] = jnp.zeros(acc_ref.shape, F32)
    n_diag = tq // tk if tq > tk else 1

    def step(j, masked):
        def chain(sub, h):
            use_mask = masked and not (single and j * tk + (sub + 1) * sw <= past)
            rows = (pl.ds(j * tk + sub * sw, sw) if isinstance(j, int)
                    else pl.ds(pl.multiple_of(j * tk + sub * sw, sw), sw))
            rs = slice(sub * sw, tq) if masked and not single and tq == tk else slice(0, tq)
            if use_mask:
                mask = j * tk + sub * sw + lax.broadcasted_iota(jnp.int32, (1, sw), 1) < q_pos[rs]
            pr = slice(LANES * (h // 2), LANES * (h // 2 + 1))
            qh = _lane_group(q_ref[0, rs, pr], HEAD_DIM * (h % 2), HEAD_DIM * (h % 2 + 1))
            z = _dot_nt(qh, k_ref[0, rows, pr])
            yield
            sp = jnp.maximum(z, 0.0) + jnp.log(1.0 + jnp.exp2(jnp.abs(z) * -LOG2E))
            if use_mask:
                sp = jnp.where(mask, sp, 0.0)
            hi, lo = _split2(sp)
            yield
            later = _dot(jnp.concatenate([hi, lo], axis=1), after)
            yield
            right = right_ref[h, rs]
            a = jnp.exp(z - sp - later - _lane_tile(right, sw // LANES))
            if use_mask:
                a = jnp.where(mask, a, 0.0)
            right_ref[h, rs] = right + jnp.sum(sp, axis=1, keepdims=True)
            yield
            acc_ref[h, rs] += _dot(a.astype(BF16), v_ref[0, rows, pr])

        _in_turns(chain(sub, h) for sub in reversed(range(tk // sw)) for h in range(N_HEADS))

    for r in reversed(range(n_diag)):
        step(n_full + r, True)
    if isinstance(n_full, int):
        for j in reversed(range(n_full)):
            step(j, False)
    else:
        def more(c):
            return (c[0] < n_full) & c[1]

        def body(c):
            step(n_full - 1 - c[0], False)
            return c[0] + 1, jnp.min(right_ref[...]) < SB_CUTOFF
        lax.while_loop(more, body, (jnp.int32(0), jnp.min(right_ref[...]) < SB_CUTOFF))
    lane = lax.broadcasted_iota(jnp.int32, (1, LANES), 1)
    for pr in range(2):
        o_ref[0, :, LANES * pr:LANES * (pr + 1)] = jnp.where(
            lane < 64, acc_ref[2 * pr], acc_ref[2 * pr + 1]).astype(o_ref.dtype)


def _check_tiles(Tq, Tk, tq, tk, past):
    single = Tq == tq
    assert Tq % tq == 0 and tq % CHUNK == 0 and Tk % tk == 0 and past + Tq <= Tk
    if single:
        n_full = past // tk
        assert (n_full + max(tq // tk, 1)) * tk >= past + Tq
    else:
        assert tq % tk == 0 and past % tk == 0
    return single


def _fox_attention(q, k, v, bounds, *, tq, tk, past):
    B, Tq, _ = q.shape
    Tk = k.shape[1]
    single = _check_tiles(Tq, Tk, tq, tk, past)
    assert single or Tk // tk <= LANES
    keys = pl.BlockSpec((1, Tk, 4 * LANES), lambda b, i: (b, 0, 0), pipeline_mode=pl.Buffered(1))
    extra_specs = [] if single else [pl.BlockSpec((1, N_HEADS, 8, LANES), lambda b, i: (b, 0, 0, 0))]
    extra = [] if single else [bounds]
    return pl.pallas_call(
        functools.partial(_fox_kernel, tq=tq, tk=tk, past=past, single=single),
        grid=(B, Tq // tq),
        in_specs=[pl.BlockSpec((1, tq, 4 * LANES), lambda b, i: (b, i, 0)), keys, keys] + extra_specs,
        out_specs=pl.BlockSpec((1, tq, 2 * LANES), lambda b, i: (b, i, 0)),
        out_shape=jax.ShapeDtypeStruct((B, Tq, 2 * LANES), BF16),
        scratch_shapes=[pltpu.VMEM((N_HEADS, tq, LANES), F32), pltpu.VMEM((N_HEADS, tq, LANES), F32)],
        compiler_params=_params(("parallel", "arbitrary")),
        name="fox_attention",
    )(q, k, v, *extra)


def _diff_attention(dl, subln, q, k, v, *, tq, tk, past, lam_init):
    B, Tq, _ = q.shape
    Tk = k.shape[1]
    single = _check_tiles(Tq, Tk, tq, tk, past)
    blk = pl.BlockSpec((1, tq, LANES), lambda b, p, i: (b, i, p))
    keys = pl.BlockSpec((1, Tk, LANES), lambda b, p, i: (b, 0, p))
    values = pl.BlockSpec((1, Tk, 2 * LANES), lambda b, p, i: (b, 0, p))
    return pl.pallas_call(
        functools.partial(_diff_kernel, tq=tq, tk=tk, past=past, single=single, lam_init=lam_init),
        grid=(B, 2, Tq // tq),
        in_specs=[pl.BlockSpec((4, DIFF_DK), lambda b, p, i: (0, 0)),
                  pl.BlockSpec((1, LANES), lambda b, p, i: (0, 0)), blk, keys, values],
        out_specs=blk,
        out_shape=jax.ShapeDtypeStruct((B, Tq, 2 * LANES), BF16),
        scratch_shapes=[pltpu.VMEM((4, tq, LANES), F32), pltpu.VMEM((4, tq, LANES), F32)],
        compiler_params=_params(("parallel", "parallel", "arbitrary")),
        name="diff_attention",
    )(dl, subln, q, k, v)


def _sb_attention(q, k, v, *, tq, tk, sw, past):
    B, Tq, _ = q.shape
    Tk = k.shape[1]
    single = _check_tiles(Tq, Tk, tq, tk, past)
    assert tk % sw == 0
    kv = pl.BlockSpec((1, Tk, 2 * LANES), lambda b, i: (b, 0, 0))
    blk = pl.BlockSpec((1, tq, 2 * LANES), lambda b, i: (b, i, 0))
    return pl.pallas_call(
        functools.partial(_sb_kernel, tq=tq, tk=tk, sw=sw, past=past, single=single),
        grid=(B, Tq // tq),
        in_specs=[blk, kv, kv],
        out_specs=blk,
        out_shape=jax.ShapeDtypeStruct((B, Tq, 2 * LANES), BF16),
        scratch_shapes=[pltpu.VMEM((N_HEADS, tq, LANES), F32), pltpu.VMEM((N_HEADS, tq, LANES), F32)],
        compiler_params=_params(("parallel", "arbitrary")),
        name="sb_attention",
    )(q, k, v)


def _mm3(a2, b2, dims=_NN):
    (ah, al), (bh, bl) = a2, b2
    dg = lambda x, y: lax.dot_general(x, y, dims, preferred_element_type=F32)
    return dg(ah, bh) + (dg(ah, bl) + dg(al, bh))


def _head_blocks(x):
    lane_head = lax.broadcasted_iota(jnp.int32, (1, 256), 1) // HEAD_DIM
    return jnp.concatenate([jnp.where(lane_head == h, x, 0.0) for h in range(N_HEADS)], axis=0)


def _fold_blocks(x):
    C = CHUNK
    return (x[0:C] + x[C:2 * C]) + (x[2 * C:3 * C] + x[3 * C:4 * C])


def _gdn_kernel(x_ref, z_ref, sm_ref, cw_ref, buf_ref, s0_ref, ng_ref,
                o_ref, sout_ref, cout_ref,
                xs_ref, qkv_ref, u_ref, w_ref, qe_ref, ke_ref, qk_ref, gl_ref, st_ref, *, tb):
    t = pl.program_id(1)
    C = CHUNK
    W = N_HEADS * HEAD_DIM
    tail = CONV_WIDTH - 1
    n_chunks = tb // C

    ri = lax.broadcasted_iota(jnp.int32, (W, W), 0)
    rj = lax.broadcasted_iota(jnp.int32, (W, W), 1)
    same_head = ri // HEAD_DIM == rj // HEAD_DIM
    same_head_b = same_head.astype(BF16)
    incl_blk = same_head & (ri % C >= rj % C)
    strict_blk = same_head & (ri % C > rj % C)
    eye = (ri == rj).astype(F32)

    @pl.when(t == 0)
    def _():
        for h in range(N_HEADS):
            st_ref[C * h:C * (h + 1), :] = jnp.concatenate(
                [s0_ref[0, h] if g == h else jnp.zeros((C, HEAD_DIM), F32) for g in range(N_HEADS)], axis=1)
        xs_ref[0:8, :] = jnp.zeros((8, GDN_CONV_CH), F32)
        xs_ref[8 - tail:8, :] = buf_ref[0]

    @pl.when(t > 0)
    def _():
        xs_ref[0:8, :] = xs_ref[tb:tb + 8, :]

    xs_ref[8:, :] = x_ref[0]
    y = cw_ref[0:1, :] * xs_ref[8 - tail:8 - tail + tb, :]
    for i in range(1, CONV_WIDTH):
        y = y + cw_ref[i:i + 1, :] * xs_ref[8 - tail + i:8 - tail + i + tb, :]
    y = y * _sigmoid(y)
    q = y[:, 0:W]
    k = y[:, W:2 * W]
    qkv_ref[:, 0:W] = q * lax.rsqrt(_dot_sel(q * q, same_head_b) + EPS) * (HEAD_DIM ** -0.5)
    qkv_ref[:, W:2 * W] = k * lax.rsqrt(_dot_sel(k * k, same_head_b) + EPS)
    qkv_ref[:, 2 * W:3 * W] = y[:, 2 * W:3 * W]

    ci = lax.broadcasted_iota(jnp.int32, (C, C), 0)
    cj = lax.broadcasted_iota(jnp.int32, (C, C), 1)
    incl_b = (ci >= cj).astype(BF16)
    li = lax.broadcasted_iota(jnp.int32, (LANES, W), 0)
    lj = lax.broadcasted_iota(jnp.int32, (LANES, W), 1) // HEAD_DIM
    spread_g = (li == 4 + lj).astype(BF16)
    spread_beta = (li == 8 + lj).astype(BF16)
    token_is_lane = (lax.broadcasted_iota(jnp.int32, (C, W), 0)
                     == lax.broadcasted_iota(jnp.int32, (C, W), 1) % C).astype(F32)
    ones_rows = jnp.ones((W, C), BF16)

    def prepare(c):
        rows = pl.ds(c * C, C) if isinstance(c, int) else pl.ds(pl.multiple_of(c * C, C), C)
        sm = sm_ref[0, rows, :]
        gcum = _dot_sel(_dot_sel(sm, incl_b, x_first=False), spread_g)
        beta = _dot_sel(sm, spread_beta)
        qn = qkv_ref[rows, 0:W]
        kn = qkv_ref[rows, W:2 * W]
        vn = qkv_ref[rows, 2 * W:3 * W]
        g_of_col = _dot_sel(gcum * token_is_lane, ones_rows, x_first=False)
        g_of_row = jnp.concatenate([gcum] * N_HEADS, axis=0)
        decay = jnp.where(incl_blk, jnp.exp(jnp.where(incl_blk, g_of_row - g_of_col, 0.0)), 0.0)
        kb = kn * beta
        k16 = _head_blocks(kn).astype(BF16)
        m = jnp.where(strict_blk, _dot_nt(_head_blocks(kb).astype(BF16), k16) * decay, 0.0)
        pw16 = (-m).astype(BF16)
        x = eye - m
        yield
        for _ in range(5):
            pw16 = _dot(pw16, pw16).astype(BF16)
            yield
            x = x + _dot(x.astype(BF16), pw16)
            yield
        x2 = _split2(x)
        res = eye - _mm3(_split2(eye + m), x2)
        yield
        inv = x + _mm3(x2, _split2(res))
        yield
        inv2 = _split2(inv)
        eg = jnp.exp(gcum)
        g_last = gcum[C - 1:C, :]
        u_ref[rows, :] = _fold_blocks(_mm3(inv2, _split2(_head_blocks(vn * beta))))
        w_ref[rows, :] = _fold_blocks(_mm3(inv2, _split2(_head_blocks(kb * eg))))
        qk_ref[rows, :] = _fold_blocks(
            jnp.where(incl_blk, _dot_nt(_head_blocks(qn).astype(BF16), k16) * decay, 0.0))
        qe_ref[rows, :] = qn * eg
        ke_ref[rows, :] = kn * jnp.exp(g_last - gcum)
        gl_rows = pl.ds(c * 8, 8) if isinstance(c, int) else pl.ds(pl.multiple_of(c * 8, 8), 8)
        gl_ref[gl_rows, :] = jnp.broadcast_to(jnp.exp(g_last), (8, W))

    def prepare_together(chunks):
        pending = [prepare(c) for c in chunks]
        while pending:
            pending = [g for g in pending if next(g, "done") is None]

    group = 8 if n_chunks % 8 == 0 else 4 if n_chunks % 4 == 0 else 1
    if n_chunks == group or group == 1 and n_chunks < 4:
        prepare_together(range(n_chunks))
    else:
        def prep_group(i, carry):
            prepare_together([group * i + r for r in range(group)])
            return carry
        lax.fori_loop(0, n_chunks // group, prep_group, 0)

    def scan(c, S):
        rows = pl.ds(c * C, C) if isinstance(c, int) else pl.ds(pl.multiple_of(c * C, C), C)
        gl_rows = pl.ds(c * 8, 8) if isinstance(c, int) else pl.ds(pl.multiple_of(c * 8, 8), 8)
        S16 = S.astype(BF16)
        v_new = u_ref[rows, :] - _dot(w_ref[rows, :].astype(BF16), S16)
        v16 = v_new.astype(BF16)
        o = _dot(qe_ref[rows, :].astype(BF16), S16) + _dot(qk_ref[rows, :].astype(BF16), _head_blocks(v16))
        S = S * gl_ref[gl_rows, :][0:1, :] + jnp.where(
            same_head, lax.dot_general(ke_ref[rows, :].astype(BF16), v16, _TN, preferred_element_type=F32), 0.0)
        ms = _dot_sel(o * o, same_head_b) * (1.0 / HEAD_DIM)
        zn = z_ref[0, rows, :]
        o_ref[0, rows, :] = (o * lax.rsqrt(ms + EPS) * ng_ref[...] * (zn * _sigmoid(zn))).astype(o_ref.dtype)
        return S

    S = st_ref[...]
    if n_chunks == 1:
        S = scan(0, S)
    else:
        S = lax.fori_loop(0, n_chunks, scan, S)
    st_ref[...] = S

    @pl.when(t == pl.num_programs(1) - 1)
    def _():
        for h in range(N_HEADS):
            sout_ref[0, h] = S[C * h:C * (h + 1), HEAD_DIM * h:HEAD_DIM * (h + 1)]
        cout_ref[0] = xs_ref[8 + tb - tail:8 + tb, :]


def _gdn(x, z, small, conv_w, buf, s0, norm_g, tb):
    B, T, _ = x.shape
    assert T % tb == 0 and tb % CHUNK == 0 and tb >= 8
    W = N_HEADS * HEAD_DIM
    tok = lambda w: pl.BlockSpec((1, tb, w), lambda b, t: (b, t, 0))
    per_b = lambda s: pl.BlockSpec((1,) + s, lambda b, t: (b,) + (0,) * len(s))
    natural = pltpu.VMEM((tb, W), F32)
    return pl.pallas_call(
        functools.partial(_gdn_kernel, tb=tb),
        grid=(B, T // tb),
        in_specs=[tok(768), tok(256), tok(128),
                  pl.BlockSpec((CONV_WIDTH, 768), lambda b, t: (0, 0)),
                  per_b((CONV_WIDTH - 1, 768)), per_b((N_HEADS, 64, 64)),
                  pl.BlockSpec((1, W), lambda b, t: (0, 0))],
        out_specs=[tok(256), per_b((N_HEADS, 64, 64)), per_b((CONV_WIDTH - 1, 768))],
        out_shape=[jax.ShapeDtypeStruct((B, T, 256), BF16),
                   jax.ShapeDtypeStruct((B, N_HEADS, 64, 64), F32),
                   jax.ShapeDtypeStruct((B, CONV_WIDTH - 1, 768), F32)],
        scratch_shapes=[pltpu.VMEM((tb + 8, 768), F32), pltpu.VMEM((tb, 768), F32)] + [natural] * 5
                       + [pltpu.VMEM((tb // CHUNK * 8, W), F32), pltpu.VMEM((W, W), F32)],
        compiler_params=_params(("parallel", "arbitrary")),
        name="gated_deltanet",
    )(x, z, small, conv_w, buf, s0, norm_g)


def _merge_kernel(x_ref, g1_ref, g2_ref, of_ref, od_ref, os_ref, og_ref, wgate_ref, wb_ref, wo_ref,
                  x1_ref, h_ref):
    x = x_ref[...]
    u = _rms(x, g1_ref[...]).astype(BF16)
    merged = None
    for i, o_ref in enumerate((of_ref, od_ref, os_ref, og_ref)):
        gate = _sigmoid(_dot(u, wgate_ref[:, i * D_MODEL:(i + 1) * D_MODEL]))
        term = gate * _dot(o_ref[...], wb_ref[i])
        merged = term if merged is None else merged + term
    x1 = x + _dot(merged.astype(BF16), wo_ref[...])
    x1_ref[...] = x1
    h_ref[...] = _rms(x1, g2_ref[...]).astype(BF16)


def _merge(x2, g1, g2, o_fox, o_diff, o_sb, o_gdn, w_gate, w_branch, w_out, tm):
    M = x2.shape[0]
    row = lambda w: pl.BlockSpec((tm, w), lambda i: (i, 0))
    return pl.pallas_call(
        _merge_kernel,
        grid=(M // tm,),
        in_specs=[row(D_MODEL), _const_spec((1, D_MODEL)), _const_spec((1, D_MODEL)),
                  row(256), row(256), row(256), row(256),
                  _const_spec((D_MODEL, N_BRANCH * D_MODEL)), _const_spec((N_BRANCH, BRANCH_W, D_MODEL)),
                  _const_spec((D_MODEL, D_MODEL))],
        out_specs=[row(D_MODEL), row(D_MODEL)],
        out_shape=[jax.ShapeDtypeStruct((M, D_MODEL), F32), jax.ShapeDtypeStruct((M, D_MODEL), BF16)],
        compiler_params=_params(("parallel",)),
        name="branch_merge",
    )(x2, g1, g2, o_fox, o_diff, o_sb, o_gdn, w_gate, w_branch, w_out)


def _ffn_kernel(x1_ref, h_ref, wg_ref, wu_ref, wd_ref, gf_ref, o_ref, *, final):
    h = h_ref[...]
    acc = x1_ref[...]
    half = D_FF // 2
    for f0 in (0, half):
        a = _dot(h, wg_ref[:, f0:f0 + half])
        b = _dot(h, wu_ref[:, f0:f0 + half])
        acc = acc + _dot((a * _sigmoid(a) * b).astype(BF16), wd_ref[f0:f0 + half, :])
    if final:
        acc = _rms(acc, gf_ref[...])
    o_ref[...] = acc


def _ffn(x1, h, wg, wu, wd, g_final, final, tm):
    M = x1.shape[0]
    row = pl.BlockSpec((tm, D_MODEL), lambda i: (i, 0))
    return pl.pallas_call(
        functools.partial(_ffn_kernel, final=final),
        grid=(M // tm,),
        in_specs=[row, row, _const_spec((D_MODEL, D_FF)), _const_spec((D_MODEL, D_FF)),
                  _const_spec((D_FF, D_MODEL)), _const_spec((1, D_MODEL))],
        out_specs=row,
        out_shape=jax.ShapeDtypeStruct((M, D_MODEL), F32),
        compiler_params=_params(("parallel",)),
        name="swiglu",
    )(x1, h, wg, wu, wd, g_final)


def _transpose_cast_kernel(x_ref, o_ref):
    o_ref[...] = x_ref[...].T.astype(o_ref.dtype)


def _transpose_cast(wt):
    n, k = wt.shape
    assert n % LANES == 0
    return pl.pallas_call(
        _transpose_cast_kernel, grid=(n // LANES,),
        in_specs=[pl.BlockSpec((LANES, k), lambda i: (i, 0))],
        out_specs=pl.BlockSpec((k, LANES), lambda i: (0, i)),
        out_shape=jax.ShapeDtypeStruct((k, n), BF16),
        compiler_params=_params(("parallel",)), name="transpose_cast",
    )(wt)


def _prep_layer(p, l):
    wt = jnp.swapaxes(p["w_in"][l], 0, 1)
    cols = lambda a, b: wt[a:b]
    small = jnp.concatenate([cols(_FOX0 + 768, _FOX0 + 772), cols(_GDN0 + 768, _GDN0 + 776),
                             jnp.zeros((LANES - 12, D_MODEL), F32)], axis=0)
    w_r = jnp.concatenate([cols(_FOX0, _FOX0 + 768), cols(_DIF0, _DIF0 + 768), cols(_SB0, _SB0 + 768),
                           cols(_GDN0, _GDN0 + 768), cols(_GDN0 + 776, _GDN0 + 1032), small], axis=0)
    sp = jnp.zeros((8, LANES), F32)
    sp = sp.at[0, 0:4].set(p["b_fox_f"][l]).at[0, 4:8].set(p["gdn_dt_bias"][l]).at[1, 4:8].set(p["gdn_a_log"][l])
    wkt = jnp.concatenate([cols(_FOX0 + 256, _FOX0 + 768), cols(_DIF0 + 256, _DIF0 + 768),
                           cols(_SB0 + 256, _SB0 + 768)], axis=0).astype(BF16)
    return dict(
        w_r=_transpose_cast(w_r), wkt=wkt, sp=sp, w_gate=_transpose_cast(cols(_GATE0, _GATE0 + N_BRANCH * D_MODEL)),
        g1=p["norm1_g"][l].reshape(1, D_MODEL), g2=p["norm2_g"][l].reshape(1, D_MODEL),
        dl=jnp.stack([p["diff_lq1"][l], p["diff_lk1"][l], p["diff_lq2"][l], p["diff_lk2"][l]]),
        subln=jnp.tile(p["diff_subln_g"][l], 2).reshape(1, LANES),
        conv_w=p["gdn_conv_w"][l], norm_g=jnp.tile(p["gdn_norm_g"][l], N_HEADS).reshape(1, N_HEADS * HEAD_DIM),
        w_branch=p["w_branch"][l].astype(BF16), w_out=p["w_out"][l].astype(BF16),
        wg=p["w_ffn_gate"][l].astype(BF16), wu=p["w_ffn_up"][l].astype(BF16), wd=p["w_ffn_down"][l].astype(BF16),
    )


def _with_past(past, new16, t_pad):
    B, T, W = new16.shape
    parts = [new16] if past is None else [past.reshape(B, -1, W).astype(BF16), new16]
    n = sum(a.shape[1] for a in parts)
    if t_pad > n:
        parts.append(jnp.zeros((B, t_pad - n, W), BF16))
    return jnp.concatenate(parts, axis=1) if len(parts) > 1 else parts[0]


def _trunk(x, caches, p, layers, *, tm, tq, tk, sw, tb):
    B, T, _ = x.shape
    M = B * T
    past = 0 if caches is None else caches[0].shape[2]
    t_keys = -(-(past + T) // tk) * tk
    x2 = x.reshape(M, D_MODEL)
    time_minor = T % tm == 0 and T >= LANES
    new_state = []
    for l in range(DEPTH):
        lp = layers[l]
        c = None if caches is None else tuple(a[l] for a in caches)
        (fq, fk32, fk16, fv32, fv16, dq, dk32, dk16, dv32, dv16, sq, sk32, sk16, sv32, sv16,
         gqkv, gz, small) = _in_proj(x2, lp["g1"], lp["w_r"], lp["sp"], tm,
                                     lp["wkt"] if time_minor else None, B if time_minor else 1)
        b3 = lambda a: a.reshape(B, T, a.shape[-1])
        kv = lambda i, a: _with_past(None if c is None else c[i], b3(a), t_keys)

        logf = b3(small)[:, :, 0:N_HEADS]
        lf_all = logf if c is None else jnp.concatenate([c[2], logf], axis=1)
        r_pad = -(-t_keys // (16 * LANES)) * 16
        lf_all = jnp.pad(lf_all, ((0, 0), (0, r_pad * LANES - lf_all.shape[1]), (0, 0)))
        cum3 = _cumsum_time(jnp.swapaxes(lf_all, 1, 2).reshape(B, N_HEADS, r_pad, LANES))
        fk_c, key_norm2 = _head_pack(kv(0, fk16), cum3, "k")
        bounds = None
        if T > tq:
            assert tk == 512 and tk % LANES == 0
            nck = t_keys // tk
            norm = lax.cummax(jnp.sqrt(key_norm2[:, :, 0, 0:N_HEADS] * 1.03), axis=1)
            c_end = sum(a[:, :, tk // LANES - 1::tk // LANES, LANES - 1] for a in cum3)[:, :, :nck]
            pad = lambda a, fill: jnp.pad(a, ((0, 0), (0, 0), (0, LANES - nck)), constant_values=fill)
            rows = jnp.stack([pad(jnp.swapaxes(norm, 1, 2), 0.0), pad(c_end, -NEG_BIG)], axis=2)
            bounds = jnp.pad(rows, ((0, 0), (0, 0), (0, 6), (0, 0)))
        o_fox = _fox_attention(_head_pack(b3(fq), cum3, "q", past), fk_c, _head_pack(kv(1, fv16), None, "v"),
                               bounds, tq=tq, tk=tk, past=past)

        lam_init = 0.8 - 0.6 * math.exp(-0.3 * l)
        o_diff = _diff_attention(lp["dl"], lp["subln"], b3(dq), kv(3, dk16), _head_pack(kv(4, dv16), None, "v"),
                                 tq=tq, tk=tk, past=past, lam_init=lam_init)

        o_sb = _sb_attention(b3(sq), kv(5, sk16), kv(6, sv16), tq=tq, tk=tk, sw=sw, past=past)

        buf = jnp.zeros((B, CONV_WIDTH - 1, GDN_CONV_CH), F32) if c is None else c[8]
        s0 = jnp.zeros((B, N_HEADS, 64, 64), F32) if c is None else c[7]
        o_gdn, s_new, conv_state = _gdn(b3(gqkv), b3(gz), b3(small), lp["conv_w"], buf, s0, lp["norm_g"], tb)

        x1, h = _merge(x2, lp["g1"], lp["g2"], o_fox.reshape(M, 256), o_diff.reshape(M, 256),
                       o_sb.reshape(M, 256), o_gdn.reshape(M, 256), lp["w_gate"], lp["w_branch"], lp["w_out"], tm)
        x2 = _ffn(x1, h, lp["wg"], lp["wu"], lp["wd"], p["final_norm_g"].reshape(1, D_MODEL),
                  l == DEPTH - 1, tm)
        if time_minor:
            h4 = lambda a: jnp.transpose(a.reshape(B, N_HEADS, HEAD_DIM, T), (0, 3, 1, 2))
        else:
            h4 = lambda a: a.reshape(B, T, N_HEADS, HEAD_DIM)
        new_state.append((h4(fk32), h4(fv32), logf, h4(dk32), h4(dv32), h4(sk32), h4(sv32), s_new, conv_state))
    y = x2.reshape(B, T, D_MODEL)
    return y, tuple(jnp.stack(parts) for parts in zip(*new_state))


def kernel(x_prompt, x_sample, cache_fox_k, cache_fox_v, cache_fox_logf, cache_diff_k, cache_diff_v, cache_sb_k, cache_sb_v, state_gdn, state_gdn_conv, norm1_g, w_in, b_fox_f, diff_lq1, diff_lk1, diff_lq2, diff_lk2, diff_subln_g, gdn_conv_w, gdn_a_log, gdn_dt_bias, gdn_norm_g, w_branch, w_out, norm2_g, w_ffn_gate, w_ffn_up, w_ffn_down, final_norm_g):
    p = dict(norm1_g=norm1_g, w_in=w_in, b_fox_f=b_fox_f, diff_lq1=diff_lq1, diff_lk1=diff_lk1, diff_lq2=diff_lq2,
             diff_lk2=diff_lk2, diff_subln_g=diff_subln_g, gdn_conv_w=gdn_conv_w, gdn_a_log=gdn_a_log,
             gdn_dt_bias=gdn_dt_bias, gdn_norm_g=gdn_norm_g, w_branch=w_branch, w_out=w_out, norm2_g=norm2_g,
             w_ffn_gate=w_ffn_gate, w_ffn_up=w_ffn_up, w_ffn_down=w_ffn_down, final_norm_g=final_norm_g)
    layers = [_prep_layer(p, l) for l in range(DEPTH)]
    t_p = x_prompt.shape[1]
    blk = min(512, t_p)
    y_prompt, sp = _trunk(x_prompt, None, p, layers, tm=blk, tq=blk, tk=blk, sw=min(256, blk), tb=blk)
    caches = (cache_fox_k, cache_fox_v, cache_fox_logf, cache_diff_k, cache_diff_v, cache_sb_k, cache_sb_v,
              state_gdn, state_gdn_conv)
    b_s, t_s = x_sample.shape[0], x_sample.shape[1]
    keys_s = -(-(cache_fox_k.shape[2] + t_s) // LANES) * LANES
    y_sample, ss = _trunk(x_sample, caches, p, layers, tm=min(512, b_s * t_s), tq=t_s, tk=keys_s, sw=LANES, tb=t_s)
    return (y_prompt, y_sample) + sp + ss
```
